```python
import math
import jax, jax.numpy as jnp
from jax import lax
import numpy as np

D_MODEL = 1024
BATCH = 8
SEQ = 4096
DEPTH = 2

HEAD_DIM = 64
SB_HEADS = 8
DIFF_HEADS = 4
SB_WIDTH = SB_HEADS * HEAD_DIM
DIFF_QK_WIDTH = DIFF_HEADS * 2 * HEAD_DIM
DIFF_VDIM = 2 * HEAD_DIM
DIFF_WIDTH = DIFF_HEADS * DIFF_VDIM
MIX_WIDTH = SB_WIDTH + DIFF_WIDTH
QKV_WIDTH = 3 * SB_WIDTH + 2 * DIFF_QK_WIDTH + DIFF_WIDTH
SPLITS = [SB_WIDTH, 2 * SB_WIDTH, 3 * SB_WIDTH,
          3 * SB_WIDTH + DIFF_QK_WIDTH, 3 * SB_WIDTH + 2 * DIFF_QK_WIDTH]
Q_BLOCK = 128
N_BUCKETS = 32
MAX_DISTANCE = 128
D_FF_DENSE = 2816
N_EXPERTS = 8
TOP_K = 2
D_FF_EXPERT = 3584
MOE_BLOCK = 128
N_DENSE = (DEPTH + 1) // 2
N_MOE = DEPTH // 2
NORM_EPS = 1e-6

kernel_name = "hybrid_stickbreak_diffattn_moe"


def rms_norm(x, g):
    xf = x.astype(jnp.float32)
    y = xf * lax.rsqrt(jnp.mean(xf * xf, axis=-1, keepdims=True) + NORM_EPS)
    return (y * g.astype(jnp.float32)).astype(x.dtype)


def t5_causal_bucket(n):
    max_exact = N_BUCKETS // 2
    nf = jnp.maximum(n, 1).astype(jnp.float32)
    large = max_exact + (jnp.log(nf / max_exact) / math.log(MAX_DISTANCE / max_exact)
                         * (N_BUCKETS - max_exact)).astype(jnp.int32)
    large = jnp.minimum(large, N_BUCKETS - 1)
    return jnp.where(n < max_exact, n, large)


def stick_breaking_attention(q, k, v):
    S = q.shape[1]
    scale = HEAD_DIM ** -0.5
    outs = []
    for blk in range(S // Q_BLOCK):
        q0 = blk * Q_BLOCK
        kl = q0 + Q_BLOCK
        z = jnp.einsum('bqhd,bkhd->bhqk', q[:, q0:kl], k[:, :kl]).astype(jnp.float32) * scale
        t_pos = q0 + jnp.arange(Q_BLOCK)[:, None]
        s_pos = jnp.arange(kl)[None, :]
        mask = s_pos < t_pos
        log_stay = jnp.where(mask, jax.nn.log_sigmoid(-z), 0.0)
        between = lax.cumsum(log_stay, axis=3, reverse=True) - log_stay
        a = jnp.where(mask, jnp.exp(jax.nn.log_sigmoid(z) + between), 0.0)
        outs.append(jnp.einsum('bhqk,bkhd->bqhd', a.astype(v.dtype), v[:, :kl]))
    return jnp.concatenate(outs, axis=1)


def differential_attention(q, k, v, bias_by_dist, lam):
    S = q.shape[1]
    scale = HEAD_DIM ** -0.5
    outs = []
    for blk in range(S // Q_BLOCK):
        q0 = blk * Q_BLOCK
        kl = q0 + Q_BLOCK
        z = jnp.einsum('bqhcd,bkhcd->bhcqk', q[:, q0:kl], k[:, :kl]).astype(jnp.float32) * scale
        dist = (q0 + jnp.arange(Q_BLOCK)[:, None]) - jnp.arange(kl)[None, :]
        bias = jnp.take(bias_by_dist, jnp.maximum(dist, 0), axis=0)
        z = z + jnp.transpose(bias, (2, 0, 1))[None, :, None]
        z = jnp.where(dist >= 0, z, -jnp.inf)
        p = jax.nn.softmax(z, axis=-1)
        w = p[:, :, 0] - lam * p[:, :, 1]
        outs.append(jnp.einsum('bhqk,bkhe->bqhe', w.astype(v.dtype), v[:, :kl]))
    return jnp.concatenate(outs, axis=1)


def swiglu(h, w_gate, w_up, w_down):
    return (jax.nn.silu(h @ w_gate) * (h @ w_up)) @ w_down


def moe_swiglu(h, router_w, w_gate, w_up, w_down):
    B, S, D = h.shape
    n = B * S
    xf = h.reshape(n, D)
    logits = jnp.dot(xf.astype(jnp.float32), router_w.astype(jnp.float32))
    top_logit, top_idx = lax.top_k(logits, TOP_K)
    gates = jax.nn.softmax(top_logit, axis=-1)
    n_assign = n * TOP_K
    expert_ids = top_idx.reshape(-1)
    token_ids = jnp.arange(n_assign, dtype=jnp.int32) // TOP_K
    order = jnp.argsort(expert_ids)
    sorted_exp = expert_ids[order]
    sorted_tok = token_ids[order]
    sorted_gate = gates.reshape(-1)[order]
    counts = jnp.bincount(expert_ids, length=N_EXPERTS)
    starts = jnp.cumsum(counts) - counts
    padded = (counts + MOE_BLOCK - 1) // MOE_BLOCK * MOE_BLOCK
    pad_ends = jnp.cumsum(padded)
    pad_starts = pad_ends - padded
    dest = pad_starts[sorted_exp] + (jnp.arange(n_assign, dtype=jnp.int32) - starts[sorted_exp])
    n_blocks = -(-n_assign // MOE_BLOCK) + N_EXPERTS
    cap = n_blocks * MOE_BLOCK
    buf = jnp.zeros((cap, D), xf.dtype).at[dest].set(xf[sorted_tok])
    block_exp = jnp.minimum(
        jnp.searchsorted(pad_ends, jnp.arange(n_blocks, dtype=jnp.int32) * MOE_BLOCK, side='right'),
        N_EXPERTS - 1)

    def expert_block(args):
        xb, e = args
        return swiglu(xb, w_gate[e], w_up[e], w_down[e])

    yb = lax.map(expert_block, (buf.reshape(n_blocks, MOE_BLOCK, D), block_exp))
    y = yb.reshape(cap, D)[dest] * sorted_gate[:, None].astype(xf.dtype)
    out = jax.ops.segment_sum(y, sorted_tok, num_segments=n)
    return out.reshape(B, S, D)


def setup_inputs(seed: int = 0) -> dict:
    key = jax.random.key(seed)
    ks = jax.random.split(key, 24)
    f32 = jnp.float32
    nrm = lambda k, shape, s: jax.random.normal(k, shape, f32) * s
    gain = lambda k, shape: 1.0 + 0.02 * jax.random.normal(k, shape, f32)
    return {
        "x": jax.random.normal(ks[0], (BATCH, SEQ, D_MODEL), f32),
        "w_in": nrm(ks[1], (DEPTH, D_MODEL, QKV_WIDTH), D_MODEL ** -0.5),
        "w_out": nrm(ks[2], (DEPTH, MIX_WIDTH, D_MODEL), MIX_WIDTH ** -0.5),
        "attn_norm": gain(ks[3], (DEPTH, D_MODEL)),
        "ffn_norm": gain(ks[4], (DEPTH, D_MODEL)),
        "sb_out_norm": gain(ks[5], (DEPTH, HEAD_DIM)),
        "diff_subln": gain(ks[6], (DEPTH, DIFF_VDIM)),
        "lambda_q1": nrm(ks[7], (DEPTH, HEAD_DIM), 0.1),
        "lambda_k1": nrm(ks[8], (DEPTH, HEAD_DIM), 0.1),
        "lambda_q2": nrm(ks[9], (DEPTH, HEAD_DIM), 0.1),
        "lambda_k2": nrm(ks[10], (DEPTH, HEAD_DIM), 0.1),
        "rel_bias": nrm(ks[11], (N_BUCKETS, DIFF_HEADS), 0.5),
        "dense_w_gate": nrm(ks[12], (N_DENSE, D_MODEL, D_FF_DENSE), D_MODEL ** -0.5),
        "dense_w_up": nrm(ks[13], (N_DENSE, D_MODEL, D_FF_DENSE), D_MODEL ** -0.5),
        "dense_w_down": nrm(ks[14], (N_DENSE, D_FF_DENSE, D_MODEL), D_FF_DENSE ** -0.5),
        "router_w": nrm(ks[15], (N_MOE, D_MODEL, N_EXPERTS), D_MODEL ** -0.5),
        "expert_w_gate": nrm(ks[16], (N_MOE, N_EXPERTS, D_MODEL, D_FF_EXPERT), D_MODEL ** -0.5),
        "expert_w_up": nrm(ks[17], (N_MOE, N_EXPERTS, D_MODEL, D_FF_EXPERT), D_MODEL ** -0.5),
        "expert_w_down": nrm(ks[18], (N_MOE, N_EXPERTS, D_FF_EXPERT, D_MODEL), D_FF_EXPERT ** -0.5),
        "final_norm": gain(ks[19], (D_MODEL,)),
    }


def reference(x, w_in, w_out, attn_norm, ffn_norm, sb_out_norm, diff_subln,
              lambda_q1, lambda_k1, lambda_q2, lambda_k2, rel_bias,
              dense_w_gate, dense_w_up, dense_w_down, router_w,
              expert_w_gate, expert_w_up, expert_w_down, final_norm):
    B, S, _ = x.shape
    bias_by_dist = rel_bias.astype(jnp.float32)[t5_causal_bucket(jnp.arange(S, dtype=jnp.int32))]
    for i in range(DEPTH):
        h = rms_norm(x, attn_norm[i])
        proj = jnp.einsum('bsd,de->bse', h, w_in[i])
        sb_q, sb_k, sb_v, df_q, df_k, df_v = jnp.split(proj, SPLITS, axis=-1)
        sb = stick_breaking_attention(sb_q.reshape(B, S, SB_HEADS, HEAD_DIM),
                                      sb_k.reshape(B, S, SB_HEADS, HEAD_DIM),
                                      sb_v.reshape(B, S, SB_HEADS, HEAD_DIM))
        sb = rms_norm(sb, sb_out_norm[i]).reshape(B, S, SB_WIDTH)
        lambda_init = 0.8 - 0.6 * math.exp(-0.3 * i)
        lam = (jnp.exp(jnp.sum(lambda_q1[i].astype(jnp.float32) * lambda_k1[i].astype(jnp.float32)))
               - jnp.exp(jnp.sum(lambda_q2[i].astype(jnp.float32) * lambda_k2[i].astype(jnp.float32)))
               + lambda_init)
        df = differential_attention(df_q.reshape(B, S, DIFF_HEADS, 2, HEAD_DIM),
                                    df_k.reshape(B, S, DIFF_HEADS, 2, HEAD_DIM),
                                    df_v.reshape(B, S, DIFF_HEADS, DIFF_VDIM),
                                    bias_by_dist, lam)
        df = (rms_norm(df, diff_subln[i]) * (1.0 - lambda_init)).reshape(B, S, DIFF_WIDTH)
        x = x + jnp.einsum('bse,ed->bsd', jnp.concatenate([sb, df], axis=-1), w_out[i])
        h2 = rms_norm(x, ffn_norm[i])
        if i % 2 == 0:
            j = i // 2
            x = x + swiglu(h2, dense_w_gate[j], dense_w_up[j], dense_w_down[j])
        else:
            j = i // 2
            x = x + moe_swiglu(h2, router_w[j], expert_w_gate[j], expert_w_up[j], expert_w_down[j])
    return rms_norm(x, final_norm)
```

```python
import functools
import math

import jax
import jax.numpy as jnp
from jax import lax
from jax.experimental import pallas as pl
from jax.experimental.pallas import tpu as pltpu

D_MODEL = 1024
HEAD_DIM = 64
SB_HEADS = 8
DIFF_HEADS = 4
SB_WIDTH = SB_HEADS * HEAD_DIM
DIFF_WIDTH = DIFF_HEADS * 2 * HEAD_DIM
QKV_WIDTH = 3 * SB_WIDTH + 3 * DIFF_WIDTH
N_BUCKETS = 32
MAX_DISTANCE = 128
N_EXPERTS = 8
TOP_K = 2
NORM_EPS = 1e-6

LANES = 128
VMEM_LIMIT = 56 * 1024 * 1024

ROW_TILE = 512
ATT_TILE = 256
MOE_TILE = 512
MOE_FF_TILE = 896
GATHER_ROWS = 512
SB_LOG_WEIGHT_FLOOR = -104.0

F32 = jnp.float32
BF16 = jnp.bfloat16
NT_DIMS = (((1,), (1,)), ((), ()))


def _params(*semantics):
    return pltpu.CompilerParams(dimension_semantics=semantics, vmem_limit_bytes=VMEM_LIMIT)


def _rms(x, g):
    return x * lax.rsqrt(jnp.mean(x * x, axis=-1, keepdims=True) + NORM_EPS) * g


def _norm_proj_kernel(x_ref, g_ref, w_ref, o_ref):
    h = _rms(x_ref[...], g_ref[...]).astype(BF16)
    for c in range(QKV_WIDTH // D_MODEL):
        cols = slice(c * D_MODEL, (c + 1) * D_MODEL)
        o_ref[:, cols] = jnp.dot(h, w_ref[:, cols], preferred_element_type=F32).astype(BF16)


def _norm_proj(x, g, w):
    n = x.shape[0]
    return pl.pallas_call(
        _norm_proj_kernel,
        grid=(n // ROW_TILE,),
        in_specs=[
            pl.BlockSpec((ROW_TILE, D_MODEL), lambda i: (i, 0)),
            pl.BlockSpec((1, D_MODEL), lambda i: (0, 0)),
            pl.BlockSpec((D_MODEL, QKV_WIDTH), lambda i: (0, 0), pipeline_mode=pl.Buffered(1)),
        ],
        out_specs=pl.BlockSpec((ROW_TILE, QKV_WIDTH), lambda i: (i, 0)),
        out_shape=jax.ShapeDtypeStruct((n, QKV_WIDTH), BF16),
        compiler_params=_params("parallel"),
        name="norm_proj",
    )(x, g, w)


def _sb_kernel(q_ref, k_ref, v_ref, g_ref, o_ref, acc_ref, carry_ref):
    t = ATT_TILE
    i = pl.program_id(2)
    q = q_ref[...]
    lane = lax.broadcasted_iota(jnp.int32, (1, LANES), 1)
    row = lax.broadcasted_iota(jnp.int32, (t, t), 0)
    col = lax.broadcasted_iota(jnp.int32, (t, t), 1)
    suffix_ones = (row >= col).astype(BF16)
    strictly_causal = col < row

    def tile(j, qh, diagonal):
        start = pl.multiple_of(j * t, t)
        ks = k_ref[pl.ds(start, t), :]
        vs = v_ref[pl.ds(start, t), :]
        z = lax.dot_general(qh, ks, NT_DIMS, preferred_element_type=F32)
        softplus = jnp.maximum(z, 0.0) + jnp.log(1.0 + jnp.exp(-jnp.abs(z)))
        log_stay = -softplus
        if diagonal:
            log_stay = jnp.where(strictly_causal, log_stay, 0.0)
        hi = log_stay.astype(BF16)
        lo = (log_stay - hi.astype(F32)).astype(BF16)
        incl = (jnp.dot(hi, suffix_ones, preferred_element_type=F32)
                + jnp.dot(lo, suffix_ones, preferred_element_type=F32))
        carry = carry_ref[...]
        log_a = (z - softplus) + (carry + (incl - log_stay))
        a = jnp.exp(log_a)
        if diagonal:
            a = jnp.where(strictly_causal, a, 0.0)
        acc_ref[...] += jnp.dot(a.astype(BF16), vs, preferred_element_type=F32)
        new_carry = carry + incl[:, 0:1]
        carry_ref[...] = new_carry
        return jnp.max(new_carry)

    heads = []
    for h in range(2):
        in_head = (lane >= h * HEAD_DIM) & (lane < (h + 1) * HEAD_DIM)
        qh = jnp.where(in_head, q, jnp.zeros_like(q))
        acc_ref[...] = jnp.zeros_like(acc_ref)
        carry_ref[...] = jnp.zeros_like(carry_ref)
        top = tile(i, qh, True)

        def cond(state):
            j, top_carry = state
            return (j >= 0) & (top_carry > SB_LOG_WEIGHT_FLOOR)

        def body(state, qh=qh):
            j, _ = state
            return j - 1, tile(j, qh, False)

        lax.while_loop(cond, body, (i - 1, top))
        heads.append(acc_ref[...])

    o = jnp.where(lane < HEAD_DIM, heads[0], heads[1])
    sq = o * o
    ms0 = jnp.sum(jnp.where(lane < HEAD_DIM, sq, 0.0), axis=-1, keepdims=True)
    ms1 = jnp.sum(jnp.where(lane < HEAD_DIM, 0.0, sq), axis=-1, keepdims=True)
    ms = jnp.where(lane < HEAD_DIM, ms0, ms1) * (1.0 / HEAD_DIM)
    o_ref[...] = (o * lax.rsqrt(ms + NORM_EPS) * g_ref[...]).astype(o_ref.dtype)


def _sb_attention(proj, g2):
    b, s, _ = proj.shape
    t = ATT_TILE
    pairs = SB_WIDTH // LANES
    return pl.pallas_call(
        _sb_kernel,
        grid=(b, pairs, s // t),
        in_specs=[
            pl.BlockSpec((None, t, LANES), lambda bi, p, i: (bi, i, p)),
            pl.BlockSpec((None, s, LANES), lambda bi, p, i: (bi, 0, pairs + p)),
            pl.BlockSpec((None, s, LANES), lambda bi, p, i: (bi, 0, 2 * pairs + p)),
            pl.BlockSpec((1, LANES), lambda bi, p, i: (0, 0)),
        ],
        out_specs=pl.BlockSpec((None, t, LANES), lambda bi, p, i: (bi, i, p)),
        out_shape=jax.ShapeDtypeStruct((b, s, SB_WIDTH), BF16),
        scratch_shapes=[pltpu.VMEM((t, LANES), F32), pltpu.VMEM((t, 1), F32)],
        compiler_params=_params("parallel", "parallel", "arbitrary"),
        name="sb_attention",
    )(proj, proj, proj, g2)


def _diff_kernel(q_ref, k_ref, v_ref, bias_ref, lam_ref, g_ref, o_ref,
                 m_ref, l_ref, acc_ref, *, lambda_init):
    t = ATT_TILE
    i = pl.program_id(2)
    q = q_ref[...]
    lane = lax.broadcasted_iota(jnp.int32, (1, LANES), 1)
    halves = [jnp.where(lane < HEAD_DIM, q, jnp.zeros_like(q)),
              jnp.where(lane < HEAD_DIM, jnp.zeros_like(q), q)]
    row = lax.broadcasted_iota(jnp.int32, (t, t), 0)
    col = lax.broadcasted_iota(jnp.int32, (t, t), 1)
    causal = col <= row

    def tile(j, bias, diagonal):
        start = pl.multiple_of(j * t, t)
        ks = k_ref[pl.ds(start, t), :]
        vs = v_ref[pl.ds(start, t), :]
        for c in range(2):
            z = lax.dot_general(halves[c], ks, NT_DIMS, preferred_element_type=F32)
            if bias is not None:
                z = z + bias
            if diagonal:
                z = jnp.where(causal, z, -1e30)
                m_new = jnp.max(z, axis=-1, keepdims=True)
                p = jnp.exp(z - m_new)
                l_ref[c] = jnp.sum(p, axis=-1, keepdims=True)
                acc_ref[c] = jnp.dot(p.astype(BF16), vs, preferred_element_type=F32)
            else:
                m_prev = m_ref[c]
                m_new = jnp.maximum(m_prev, jnp.max(z, axis=-1, keepdims=True))
                alpha = jnp.exp(m_prev - m_new)
                p = jnp.exp(z - m_new)
                l_ref[c] = alpha * l_ref[c] + jnp.sum(p, axis=-1, keepdims=True)
                acc_ref[c] = alpha * acc_ref[c] + jnp.dot(p.astype(BF16), vs,
                                                          preferred_element_type=F32)
            m_ref[c] = m_new

    tile(i, bias_ref[0], True)

    @pl.when(i >= 1)
    def _():
        tile(i - 1, bias_ref[1], False)

    def far(n, carry):
        tile(i - 2 - n, None, False)
        return carry

    lax.fori_loop(0, jnp.maximum(i - 1, 0), far, 0)

    lp = lam_ref[...]
    lam = (jnp.exp(jnp.sum(lp[0:1] * lp[1:2], axis=-1, keepdims=True))
           - jnp.exp(jnp.sum(lp[2:3] * lp[3:4], axis=-1, keepdims=True)) + lambda_init)
    o = acc_ref[0] / l_ref[0] - lam * (acc_ref[1] / l_ref[1])
    o_ref[...] = (_rms(o, g_ref[...]) * (1.0 - lambda_init)).astype(o_ref.dtype)


def _diff_attention(proj, bias_tiles, lam_params, g, lambda_init):
    b, s, _ = proj.shape
    t = ATT_TILE
    q0 = 3 * SB_WIDTH // LANES
    return pl.pallas_call(
        functools.partial(_diff_kernel, lambda_init=lambda_init),
        grid=(b, DIFF_HEADS, s // t),
        in_specs=[
            pl.BlockSpec((None, t, LANES), lambda bi, h, i: (bi, i, q0 + h)),
            pl.BlockSpec((None, s, LANES), lambda bi, h, i: (bi, 0, q0 + DIFF_HEADS + h)),
            pl.BlockSpec((None, s, LANES), lambda bi, h, i: (bi, 0, q0 + 2 * DIFF_HEADS + h)),
            pl.BlockSpec((None, 2, t, t), lambda bi, h, i: (h, 0, 0, 0)),
            pl.BlockSpec((4, HEAD_DIM), lambda bi, h, i: (0, 0)),
            pl.BlockSpec((1, LANES), lambda bi, h, i: (0, 0)),
        ],
        out_specs=pl.BlockSpec((None, t, LANES), lambda bi, h, i: (bi, i, h)),
        out_shape=jax.ShapeDtypeStruct((b, s, DIFF_WIDTH), BF16),
        scratch_shapes=[pltpu.VMEM((2, t, 1), F32), pltpu.VMEM((2, t, 1), F32),
                        pltpu.VMEM((2, t, LANES), F32)],
        compiler_params=_params("parallel", "parallel", "arbitrary"),
        name="diff_attention",
    )(proj, proj, proj, bias_tiles, lam_params, g)


def _t5_causal_bucket(n):
    max_exact = N_BUCKETS // 2
    nf = jnp.maximum(n, 1).astype(F32)
    large = max_exact + (jnp.log(nf / max_exact) / math.log(MAX_DISTANCE / max_exact)
                         * (N_BUCKETS - max_exact)).astype(jnp.int32)
    large = jnp.minimum(large, N_BUCKETS - 1)
    return jnp.where(n < max_exact, n, large)


def _bias_tiles(rel_bias):
    t = ATT_TILE
    dist = jnp.arange(2 * t, dtype=jnp.int32)
    bucket = _t5_causal_bucket(dist)
    assert MAX_DISTANCE <= t
    table = rel_bias.astype(F32)
    by_dist = (table[bucket] - table[N_BUCKETS - 1][None, :]).T
    rel = jnp.arange(t, dtype=jnp.int32)[:, None] - jnp.arange(t, dtype=jnp.int32)[None, :]
    diag = by_dist[:, jnp.maximum(rel, 0)]
    prev = by_dist[:, rel + t]
    return jnp.stack([diag, prev], axis=1)


def _route_top2(h, rw):
    logits = jnp.dot(h, rw, preferred_element_type=F32, precision=lax.Precision.HIGHEST)
    lane = lax.broadcasted_iota(jnp.int32, logits.shape, 1)
    neg = -jnp.inf
    lg = jnp.where(lane < N_EXPERTS, logits, neg)
    m1 = jnp.max(lg, axis=-1, keepdims=True)
    i1 = jnp.min(jnp.where(lg == m1, lane, LANES), axis=-1, keepdims=True)
    lg2 = jnp.where(lane == i1, neg, lg)
    m2 = jnp.max(lg2, axis=-1, keepdims=True)
    i2 = jnp.min(jnp.where(lg2 == m2, lane, LANES), axis=-1, keepdims=True)
    e = jnp.exp(m2 - m1)
    g1 = 1.0 / (1.0 + e)
    g2 = e * g1
    return jnp.where(lane == 0, i1.astype(F32),
                     jnp.where(lane == 1, i2.astype(F32),
                               jnp.where(lane == 2, g1, jnp.where(lane == 3, g2, 0.0))))


def _out_proj_kernel(x_ref, sb_ref, df_ref, wo_ref, g_ref, *rest, route):
    if route:
        rw_ref, xo_ref, h_ref, route_ref = rest
    else:
        xo_ref, h_ref = rest
    y = (jnp.dot(sb_ref[...], wo_ref[:SB_WIDTH, :], preferred_element_type=F32)
         + jnp.dot(df_ref[...], wo_ref[SB_WIDTH:, :], preferred_element_type=F32))
    xn = x_ref[...] + y
    xo_ref[...] = xn
    h = _rms(xn, g_ref[...])
    h_ref[...] = h.astype(BF16)
    if route:
        route_ref[...] = _route_top2(h, rw_ref[...])


def _out_proj(x, sb, df, wo, g, router_w=None):
    n = x.shape[0]
    route = router_w is not None
    rows = lambda w: pl.BlockSpec((ROW_TILE, w), lambda i: (i, 0))
    whole = lambda a: pl.BlockSpec(a.shape, lambda i: (0, 0))
    in_specs = [rows(D_MODEL), rows(SB_WIDTH), rows(DIFF_WIDTH), whole(wo), whole(g)]
    out_specs = [rows(D_MODEL), rows(D_MODEL)]
    out_shape = [jax.ShapeDtypeStruct((n, D_MODEL), F32), jax.ShapeDtypeStruct((n, D_MODEL), BF16)]
    args = [x, sb, df, wo, g]
    if route:
        in_specs.append(whole(router_w))
        out_specs.append(rows(LANES))
        out_shape.append(jax.ShapeDtypeStruct((n, LANES), F32))
        args.append(router_w)
    return pl.pallas_call(
        functools.partial(_out_proj_kernel, route=route),
        grid=(n // ROW_TILE,),
        in_specs=in_specs,
        out_specs=out_specs,
        out_shape=out_shape,
        compiler_params=_params("parallel"),
        name="out_proj_route" if route else "out_proj",
    )(*args)


def _dense_ffn_kernel(x_ref, h_ref, wg_ref, wu_ref, wd_ref, o_ref):
    h = h_ref[...]
    gate = jnp.dot(h, wg_ref[...], preferred_element_type=F32)
    up = jnp.dot(h, wu_ref[...], preferred_element_type=F32)
    act = (gate * jax.nn.sigmoid(gate) * up).astype(BF16)
    o_ref[...] = x_ref[...] + jnp.dot(act, wd_ref[...], preferred_element_type=F32)


def _dense_ffn(x, h, wg, wu, wd):
    n = x.shape[0]
    tm = ROW_TILE // 2
    rows = pl.BlockSpec((tm, D_MODEL), lambda i: (i, 0))
    whole = lambda a: pl.BlockSpec(a.shape, lambda i: (0, 0), pipeline_mode=pl.Buffered(1))
    return pl.pallas_call(
        _dense_ffn_kernel,
        grid=(n // tm,),
        in_specs=[rows, rows, whole(wg), whole(wu), whole(wd)],
        out_specs=rows,
        out_shape=jax.ShapeDtypeStruct((n, D_MODEL), F32),
        compiler_params=_params("parallel"),
        name="dense_ffn",
    )(x, h, wg, wu, wd)


def _gather_kernel(idx_ref, src_ref, dst_ref, sem):
    step = pl.program_id(0)
    base = step * GATHER_ROWS

    def start(r, carry):
        pltpu.make_async_copy(src_ref.at[idx_ref[base + r]], dst_ref.at[base + r], sem).start()
        return carry

    def wait(r, carry):
        pltpu.make_async_copy(src_ref.at[0], dst_ref.at[0], sem).wait()
        return carry

    lax.fori_loop(0, GATHER_ROWS, start, 0)

    @pl.when(step > 0)
    def _():
        lax.fori_loop(0, GATHER_ROWS, wait, 0)

    @pl.when(step == pl.num_programs(0) - 1)
    def _():
        lax.fori_loop(0, GATHER_ROWS, wait, 0)


def _gather_rows(src, idx):
    m = idx.shape[0]
    sub = D_MODEL // LANES
    out = pl.pallas_call(
        _gather_kernel,
        grid_spec=pltpu.PrefetchScalarGridSpec(
            num_scalar_prefetch=1,
            grid=(m // GATHER_ROWS,),
            in_specs=[pl.BlockSpec(memory_space=pl.ANY)],
            out_specs=pl.BlockSpec(memory_space=pl.ANY),
            scratch_shapes=[pltpu.SemaphoreType.DMA(())],
        ),
        out_shape=jax.ShapeDtypeStruct((m, sub, LANES), src.dtype),
        compiler_params=_params("arbitrary"),
        name="gather_rows",
    )(idx, src.reshape(src.shape[0], sub, LANES))
    return out.reshape(m, D_MODEL)


def _expert_kernel(be_ref, nused_ref, x_ref, wg_ref, wu_ref, wd_ref, o_ref, acc_ref):
    f = pl.program_id(1)

    @pl.when(pl.program_id(0) < nused_ref[0])
    def _():
        x = x_ref[...]
        gate = jnp.dot(x, wg_ref[...], preferred_element_type=F32)
        up = jnp.dot(x, wu_ref[...], preferred_element_type=F32)
        act = (gate * jax.nn.sigmoid(gate) * up).astype(BF16)
        y = jnp.dot(act, wd_ref[...], preferred_element_type=F32)

        @pl.when(f == 0)
        def _():
            acc_ref[...] = y

        @pl.when(f > 0)
        def _():
            acc_ref[...] += y

    @pl.when(f == pl.num_programs(1) - 1)
    def _():
        o_ref[...] = acc_ref[...].astype(o_ref.dtype)


def _expert_ffn(xs, block_expert, n_used, wg, wu, wd):
    cap = xs.shape[0]
    d_ff = wg.shape[-1]
    nf = d_ff // MOE_FF_TILE

    def ff_step(i, f, nused):
        return jnp.where(i < nused[0], f, nf - 1)

    return pl.pallas_call(
        _expert_kernel,
        grid_spec=pltpu.PrefetchScalarGridSpec(
            num_scalar_prefetch=2,
            grid=(cap // MOE_TILE, nf),
            in_specs=[
                pl.BlockSpec((MOE_TILE, D_MODEL), lambda i, f, be, nu: (i, 0)),
                pl.BlockSpec((None, D_MODEL, MOE_FF_TILE),
                             lambda i, f, be, nu: (be[i], 0, ff_step(i, f, nu))),
                pl.BlockSpec((None, D_MODEL, MOE_FF_TILE),
                             lambda i, f, be, nu: (be[i], 0, ff_step(i, f, nu))),
                pl.BlockSpec((None, MOE_FF_TILE, D_MODEL),
                             lambda i, f, be, nu: (be[i], ff_step(i, f, nu), 0)),
            ],
            out_specs=pl.BlockSpec((MOE_TILE, D_MODEL), lambda i, f, be, nu: (i, 0)),
            scratch_shapes=[pltpu.VMEM((MOE_TILE, D_MODEL), F32)],
        ),
        out_shape=jax.ShapeDtypeStruct((cap, D_MODEL), BF16),
        compiler_params=_params("arbitrary", "arbitrary"),
        name="expert_ffn",
    )(block_expert, n_used, xs, wg, wu, wd)


def _combine_kernel(x_ref, y_ref, route_ref, g_ref, o_ref, *, final):
    r = route_ref[...]
    y = y_ref[...].astype(F32)
    out = x_ref[...] + r[:, 2:3] * y[:, :D_MODEL] + r[:, 3:4] * y[:, D_MODEL:]
    if final:
        out = _rms(out, g_ref[...])
    o_ref[...] = out


def _combine(x, yg, route, g, final):
    n = x.shape[0]
    rows = lambda w: pl.BlockSpec((ROW_TILE, w), lambda i: (i, 0))
    return pl.pallas_call(
        functools.partial(_combine_kernel, final=final),
        grid=(n // ROW_TILE,),
        in_specs=[rows(D_MODEL), rows(TOP_K * D_MODEL), rows(LANES),
                  pl.BlockSpec((1, D_MODEL), lambda i: (0, 0))],
        out_specs=rows(D_MODEL),
        out_shape=jax.ShapeDtypeStruct((n, D_MODEL), F32),
        compiler_params=_params("parallel"),
        name="moe_combine",
    )(x, yg, route, g)


def _moe(x, h, route, wg, wu, wd, final_g, final):
    n = x.shape[0]
    n_assign = n * TOP_K
    n_blocks = n_assign // MOE_TILE + N_EXPERTS
    cap = n_blocks * MOE_TILE
    experts = route[:, :TOP_K].astype(jnp.int32).reshape(n_assign)
    onehot = (experts[:, None] == jnp.arange(N_EXPERTS, dtype=jnp.int32)[None, :]).astype(jnp.int32)
    running = jnp.cumsum(onehot, axis=0)
    rank = jnp.sum(running * onehot, axis=1) - 1
    counts = running[-1]
    padded = (counts + MOE_TILE - 1) // MOE_TILE * MOE_TILE
    pad_ends = jnp.cumsum(padded)
    pad_starts = pad_ends - padded
    dest = jnp.sum(onehot * pad_starts[None, :], axis=1) + rank
    token = jnp.arange(n_assign, dtype=jnp.int32) // TOP_K
    src_row = jnp.zeros((cap,), jnp.int32).at[dest].set(token)
    block_start = jnp.arange(n_blocks, dtype=jnp.int32) * MOE_TILE
    block_expert = jnp.minimum(
        jnp.sum((block_start[:, None] >= pad_ends[None, :]).astype(jnp.int32), axis=1),
        N_EXPERTS - 1).astype(jnp.int32)
    n_used = (pad_ends[-1:] // MOE_TILE).astype(jnp.int32)

    xs = _gather_rows(h, src_row)
    ys = _expert_ffn(xs, block_expert, n_used, wg, wu, wd)
    yg = _gather_rows(ys, dest.astype(jnp.int32)).reshape(n, TOP_K * D_MODEL)
    return _combine(x, yg, route, final_g, final)


def _final_norm_kernel(x_ref, g_ref, o_ref):
    o_ref[...] = _rms(x_ref[...], g_ref[...])


def _final_norm(x, g):
    n = x.shape[0]
    rows = pl.BlockSpec((ROW_TILE, D_MODEL), lambda i: (i, 0))
    return pl.pallas_call(
        _final_norm_kernel,
        grid=(n // ROW_TILE,),
        in_specs=[rows, pl.BlockSpec((1, D_MODEL), lambda i: (0, 0))],
        out_specs=rows,
        out_shape=jax.ShapeDtypeStruct((n, D_MODEL), F32),
        compiler_params=_params("parallel"),
        name="final_norm",
    )(x, g)


def kernel(x, w_in, w_out, attn_norm, ffn_norm, sb_out_norm, diff_subln, lambda_q1, lambda_k1,
           lambda_q2, lambda_k2, rel_bias, dense_w_gate, dense_w_up, dense_w_down, router_w,
           expert_w_gate, expert_w_up, expert_w_down, final_norm):
    b, s, d = x.shape
    depth = w_in.shape[0]
    n = b * s
    assert d == D_MODEL and s % ATT_TILE == 0 and n % ROW_TILE == 0
    assert (n * TOP_K) % MOE_TILE == 0 and (n * TOP_K) % GATHER_ROWS == 0

    scale = HEAD_DIM ** -0.5
    col = jnp.arange(QKV_WIDTH)
    is_q = (col < SB_WIDTH) | ((col >= 3 * SB_WIDTH) & (col < 3 * SB_WIDTH + DIFF_WIDTH))
    col_scale = jnp.where(is_q, scale, 1.0).astype(F32)
    bias_tiles = _bias_tiles(rel_bias)
    row2 = lambda v: v.astype(F32).reshape(1, -1)

    xf = x.reshape(n, d)
    for i in range(depth):
        last = i == depth - 1
        proj = _norm_proj(xf, row2(attn_norm[i]), (w_in[i] * col_scale[None, :]).astype(BF16))
        proj = proj.reshape(b, s, QKV_WIDTH)
        sb = _sb_attention(proj, row2(jnp.tile(sb_out_norm[i], 2)))
        lambda_init = 0.8 - 0.6 * math.exp(-0.3 * i)
        lam_params = jnp.stack([lambda_q1[i], lambda_k1[i], lambda_q2[i], lambda_k2[i]]).astype(F32)
        df = _diff_attention(proj, bias_tiles, lam_params, row2(diff_subln[i]), lambda_init)
        sb = sb.reshape(n, SB_WIDTH)
        df = df.reshape(n, DIFF_WIDTH)
        wo = w_out[i].astype(BF16)
        j = i // 2
        if i % 2 == 0:
            xf, h = _out_proj(xf, sb, df, wo, row2(ffn_norm[i]))
            xf = _dense_ffn(xf, h, dense_w_gate[j].astype(BF16), dense_w_up[j].astype(BF16),
                            dense_w_down[j].astype(BF16))
            if last:
                xf = _final_norm(xf, row2(final_norm))
        else:
            rw = jnp.pad(router_w[j].astype(F32), ((0, 0), (0, LANES - N_EXPERTS)))
            xf, h, route = _out_proj(xf, sb, df, wo, row2(ffn_norm[i]), rw)
            xf = _moe(xf, h, route, expert_w_gate[j].astype(BF16), expert_w_up[j].astype(BF16),
                      expert_w_down[j].astype(BF16), row2(final_norm), last)
    return xf.reshape(b, s, d)
```

```python
import functools
import math

import jax
import jax.numpy as jnp
from jax import lax
from jax.experimental import pallas as pl
from jax.experimental.pallas import tpu as pltpu

D_MODEL = 1024
HEAD_DIM = 64
SB_HEADS = 8
DIFF_HEADS = 4
SB_WIDTH = SB_HEADS * HEAD_DIM
DIFF_WIDTH = DIFF_HEADS * 2 * HEAD_DIM
QK_WIDTH = 2 * SB_WIDTH + 2 * DIFF_WIDTH
V_WIDTH = SB_WIDTH + DIFF_WIDTH
N_BUCKETS = 32
MAX_DISTANCE = 128
N_EXPERTS = 8
TOP_K = 2
NORM_EPS = 1e-6

LANES = 128
VMEM_LIMIT = 56 * 1024 * 1024

ROW_TILE = 512
ATT_TILE = 256
MOE_TILE = 512
MOE_FF_TILE = 1792
MOE_FF_STEPS = 2
SB_LOG_WEIGHT_FLOOR = -104.0

F32 = jnp.float32
BF16 = jnp.bfloat16
NT_DIMS = (((1,), (1,)), ((), ()))


def _params(*semantics):
    return pltpu.CompilerParams(dimension_semantics=semantics, vmem_limit_bytes=VMEM_LIMIT)


def _rms(x, g):
    return x * lax.rsqrt(jnp.mean(x * x, axis=-1, keepdims=True) + NORM_EPS) * g


def _norm_proj_kernel(x_ref, g_ref, wqk_ref, wvt_ref, qk_ref, vt_ref):
    h = _rms(x_ref[...], g_ref[...]).astype(BF16)
    for c in range(QK_WIDTH // D_MODEL):
        cols = slice(c * D_MODEL, (c + 1) * D_MODEL)
        qk_ref[:, cols] = jnp.dot(h, wqk_ref[:, cols], preferred_element_type=F32).astype(BF16)
    vt_ref[...] = lax.dot_general(wvt_ref[...], h, NT_DIMS,
                                  preferred_element_type=F32).astype(BF16)


def _norm_proj(x, g, wqk, wvt, batch, seq):
    n = x.shape[0]
    per_seq = seq // ROW_TILE
    whole = lambda a: pl.BlockSpec(a.shape, lambda b, i: (0, 0), pipeline_mode=pl.Buffered(1))
    return pl.pallas_call(
        _norm_proj_kernel,
        grid=(batch, per_seq),
        in_specs=[
            pl.BlockSpec((ROW_TILE, D_MODEL), lambda b, i: (b * per_seq + i, 0)),
            pl.BlockSpec((1, D_MODEL), lambda b, i: (0, 0)),
            whole(wqk), whole(wvt),
        ],
        out_specs=[
            pl.BlockSpec((ROW_TILE, QK_WIDTH), lambda b, i: (b * per_seq + i, 0)),
            pl.BlockSpec((None, V_WIDTH, ROW_TILE), lambda b, i: (b, 0, i)),
        ],
        out_shape=[jax.ShapeDtypeStruct((n, QK_WIDTH), BF16),
                   jax.ShapeDtypeStruct((batch, V_WIDTH, seq), BF16)],
        compiler_params=_params("parallel", "parallel"),
        name="norm_proj",
    )(x, g, wqk, wvt)


def _sb_kernel(q_ref, k_ref, vt_ref, g_ref, o_ref, qq_ref, acc_ref, carry_ref):
    t = ATT_TILE
    i = pl.program_id(2)
    q = q_ref[...]
    lane = lax.broadcasted_iota(jnp.int32, (1, LANES), 1)
    qq_ref[0:t, :] = jnp.where(lane < HEAD_DIM, q, jnp.zeros_like(q))
    qq_ref[t:2 * t, :] = jnp.where(lane < HEAD_DIM, jnp.zeros_like(q), q)
    key = lax.broadcasted_iota(jnp.int32, (t, t), 0)
    pos = lax.broadcasted_iota(jnp.int32, (t, t), 1)
    suffix_ones = (pos >= key).astype(BF16)
    key2 = lax.broadcasted_iota(jnp.int32, (t, 2 * t), 0)
    qry2 = lax.broadcasted_iota(jnp.int32, (t, 2 * t), 1)
    strictly_causal = key2 < jnp.where(qry2 >= t, qry2 - t, qry2)

    def tile(j, diagonal):
        start = pl.multiple_of(j * t, t)
        ks = k_ref[pl.ds(start, t), :]
        vt = vt_ref[:, pl.ds(start, t)]
        z = lax.dot_general(ks, qq_ref[...], NT_DIMS, preferred_element_type=F32)
        softplus = jnp.maximum(z, 0.0) + jnp.log(1.0 + jnp.exp(-jnp.abs(z)))
        log_stay = -softplus
        if diagonal:
            log_stay = jnp.where(strictly_causal, log_stay, 0.0)
        hi = log_stay.astype(BF16)
        lo = (log_stay - hi.astype(F32)).astype(BF16)
        incl = (jnp.dot(suffix_ones, hi, preferred_element_type=F32)
                + jnp.dot(suffix_ones, lo, preferred_element_type=F32))
        carry = carry_ref[...]
        log_a = (z - softplus) + (carry + (incl - log_stay))
        a = jnp.exp(log_a)
        if diagonal:
            a = jnp.where(strictly_causal, a, 0.0)
        update = jnp.dot(vt, a.astype(BF16), preferred_element_type=F32)
        if diagonal:
            acc_ref[...] = update
        else:
            acc_ref[...] += update
        new_carry = carry + incl[0:1, :]
        carry_ref[...] = new_carry
        return jnp.max(new_carry)

    carry_ref[...] = jnp.zeros_like(carry_ref)
    top = tile(i, True)

    def cond(state):
        j, top_carry = state
        return (j >= 0) & (top_carry > SB_LOG_WEIGHT_FLOOR)

    def body(state):
        j, _ = state
        return j - 1, tile(j, False)

    lax.while_loop(cond, body, (i - 1, top))

    normed = []
    for h in range(2):
        o = acc_ref[h * HEAD_DIM:(h + 1) * HEAD_DIM, h * t:(h + 1) * t]
        ms = jnp.mean(o * o, axis=0, keepdims=True)
        normed.append(o * lax.rsqrt(ms + NORM_EPS))
    o_ref[...] = (jnp.concatenate(normed, axis=0).T * g_ref[...]).astype(o_ref.dtype)


def _sb_attention(qk, vt, g2):
    b, s, _ = qk.shape
    t = ATT_TILE
    pairs = SB_WIDTH // LANES
    return pl.pallas_call(
        _sb_kernel,
        grid=(b, pairs, s // t),
        in_specs=[
            pl.BlockSpec((None, t, LANES), lambda bi, p, i: (bi, i, p)),
            pl.BlockSpec((None, s, LANES), lambda bi, p, i: (bi, 0, pairs + p)),
            pl.BlockSpec((None, LANES, s), lambda bi, p, i: (bi, p, 0)),
            pl.BlockSpec((1, LANES), lambda bi, p, i: (0, 0)),
        ],
        out_specs=pl.BlockSpec((None, t, LANES), lambda bi, p, i: (bi, i, p)),
        out_shape=jax.ShapeDtypeStruct((b, s, SB_WIDTH), BF16),
        scratch_shapes=[pltpu.VMEM((2 * t, LANES), BF16), pltpu.VMEM((LANES, 2 * t), F32),
                        pltpu.VMEM((1, 2 * t), F32)],
        compiler_params=_params("parallel", "parallel", "arbitrary"),
        name="sb_attention",
    )(qk, qk, vt, g2)


def _diff_kernel(q_ref, k_ref, vt_ref, bias_ref, lam_ref, g_ref, o_ref,
                 qq_ref, m_ref, l_ref, acc_ref, *, lambda_init):
    t = ATT_TILE
    i = pl.program_id(2)
    q = q_ref[...]
    lane = lax.broadcasted_iota(jnp.int32, (1, LANES), 1)
    qq_ref[0:t, :] = jnp.where(lane < HEAD_DIM, q, jnp.zeros_like(q))
    qq_ref[t:2 * t, :] = jnp.where(lane < HEAD_DIM, jnp.zeros_like(q), q)
    key2 = lax.broadcasted_iota(jnp.int32, (t, 2 * t), 0)
    qry2 = lax.broadcasted_iota(jnp.int32, (t, 2 * t), 1)
    causal = key2 <= jnp.where(qry2 >= t, qry2 - t, qry2)

    def tile(j, bias, diagonal):
        start = pl.multiple_of(j * t, t)
        ks = k_ref[pl.ds(start, t), :]
        vt = vt_ref[:, pl.ds(start, t)]
        z = lax.dot_general(ks, qq_ref[...], NT_DIMS, preferred_element_type=F32)
        if bias is not None:
            z = z + bias
        if diagonal:
            z = jnp.where(causal, z, -1e30)
            m_new = jnp.max(z, axis=0, keepdims=True)
            p = jnp.exp(z - m_new)
            l_ref[...] = jnp.sum(p, axis=0, keepdims=True)
            acc_ref[...] = jnp.dot(vt, p.astype(BF16), preferred_element_type=F32)
        else:
            m_prev = m_ref[...]
            m_new = jnp.maximum(m_prev, jnp.max(z, axis=0, keepdims=True))
            alpha = jnp.exp(m_prev - m_new)
            p = jnp.exp(z - m_new)
            l_ref[...] = alpha * l_ref[...] + jnp.sum(p, axis=0, keepdims=True)
            acc_ref[...] = alpha * acc_ref[...] + jnp.dot(vt, p.astype(BF16),
                                                          preferred_element_type=F32)
        m_ref[...] = m_new

    tile(i, bias_ref[0], True)

    @pl.when(i >= 1)
    def _():
        tile(i - 1, bias_ref[1], False)

    def far(n, carry):
        tile(i - 2 - n, None, False)
        return carry

    lax.fori_loop(0, jnp.maximum(i - 1, 0), far, 0)

    lp = lam_ref[...]
    lam = (jnp.exp(jnp.sum(lp[0:1] * lp[1:2], axis=-1, keepdims=True))
           - jnp.exp(jnp.sum(lp[2:3] * lp[3:4], axis=-1, keepdims=True)) + lambda_init)
    normalised = acc_ref[...] / l_ref[...]
    o = normalised[:, :t] - lam * normalised[:, t:]
    o = o * lax.rsqrt(jnp.mean(o * o, axis=0, keepdims=True) + NORM_EPS)
    o_ref[...] = (o.T * g_ref[...] * (1.0 - lambda_init)).astype(o_ref.dtype)


def _diff_attention(qk, vt, bias_tiles, lam_params, g, lambda_init):
    b, s, _ = qk.shape
    t = ATT_TILE
    q0 = 2 * SB_WIDTH // LANES
    v0 = SB_WIDTH // LANES
    return pl.pallas_call(
        functools.partial(_diff_kernel, lambda_init=lambda_init),
        grid=(b, DIFF_HEADS, s // t),
        in_specs=[
            pl.BlockSpec((None, t, LANES), lambda bi, h, i: (bi, i, q0 + h)),
            pl.BlockSpec((None, s, LANES), lambda bi, h, i: (bi, 0, q0 + DIFF_HEADS + h)),
            pl.BlockSpec((None, LANES, s), lambda bi, h, i: (bi, v0 + h, 0)),
            pl.BlockSpec((None, 2, t, 2 * t), lambda bi, h, i: (h, 0, 0, 0)),
            pl.BlockSpec((4, HEAD_DIM), lambda bi, h, i: (0, 0)),
            pl.BlockSpec((1, LANES), lambda bi, h, i: (0, 0)),
        ],
        out_specs=pl.BlockSpec((None, t, LANES), lambda bi, h, i: (bi, i, h)),
        out_shape=jax.ShapeDtypeStruct((b, s, DIFF_WIDTH), BF16),
        scratch_shapes=[pltpu.VMEM((2 * t, LANES), BF16), pltpu.VMEM((1, 2 * t), F32),
                        pltpu.VMEM((1, 2 * t), F32), pltpu.VMEM((LANES, 2 * t), F32)],
        compiler_params=_params("parallel", "parallel", "arbitrary"),
        name="diff_attention",
    )(qk, qk, vt, bias_tiles, lam_params, g)


def _t5_causal_bucket(n):
    max_exact = N_BUCKETS // 2
    nf = jnp.maximum(n, 1).astype(F32)
    large = max_exact + (jnp.log(nf / max_exact) / math.log(MAX_DISTANCE / max_exact)
                         * (N_BUCKETS - max_exact)).astype(jnp.int32)
    large = jnp.minimum(large, N_BUCKETS - 1)
    return jnp.where(n < max_exact, n, large)


def _bias_tiles(rel_bias):
    t = ATT_TILE
    assert MAX_DISTANCE <= t
    key = jnp.arange(t, dtype=jnp.int32)[:, None]
    qry = jnp.arange(t, dtype=jnp.int32)[None, :]
    dist = jnp.stack([jnp.maximum(qry - key, 0), qry - key + t])
    bucket = _t5_causal_bucket(dist)
    table = rel_bias.astype(F32) - rel_bias.astype(F32)[N_BUCKETS - 1][None, :]
    tiles = jnp.zeros((DIFF_HEADS,) + dist.shape, F32)
    for k in range(N_BUCKETS - 1):
        tiles = jnp.where(bucket[None] == k, table[k][:, None, None, None], tiles)
    return jnp.concatenate([tiles, tiles], axis=-1)


def _route_top2(h, rw_ref):
    h_hi = h.astype(BF16)
    h_lo = (h - h_hi.astype(F32)).astype(BF16)
    both = jnp.dot(h_hi, rw_ref[...], preferred_element_type=F32)
    logits = (both[:, :LANES] + both[:, LANES:]
              + jnp.dot(h_lo, rw_ref[:, :LANES], preferred_element_type=F32))
    lane = lax.broadcasted_iota(jnp.int32, logits.shape, 1)
    neg = -jnp.inf
    lg = jnp.where(lane < N_EXPERTS, logits, neg)
    m1 = jnp.max(lg, axis=-1, keepdims=True)
    i1 = jnp.min(jnp.where(lg == m1, lane, LANES), axis=-1, keepdims=True)
    lg2 = jnp.where(lane == i1, neg, lg)
    m2 = jnp.max(lg2, axis=-1, keepdims=True)
    i2 = jnp.min(jnp.where(lg2 == m2, lane, LANES), axis=-1, keepdims=True)
    e = jnp.exp(m2 - m1)
    g1 = 1.0 / (1.0 + e)
    g2 = e * g1
    return jnp.where(lane == 0, i1.astype(F32),
                     jnp.where(lane == 1, i2.astype(F32),
                               jnp.where(lane == 2, g1, jnp.where(lane == 3, g2, 0.0))))


def _out_proj_kernel(x_ref, sb_ref, df_ref, wo_ref, g_ref, *rest, route):
    if route:
        rw_ref, xo_ref, h_ref, route_ref = rest
    else:
        xo_ref, h_ref = rest
    y = (jnp.dot(sb_ref[...], wo_ref[:SB_WIDTH, :], preferred_element_type=F32)
         + jnp.dot(df_ref[...], wo_ref[SB_WIDTH:, :], preferred_element_type=F32))
    xn = x_ref[...] + y
    xo_ref[...] = xn
    h = _rms(xn, g_ref[...])
    h_ref[...] = h.astype(h_ref.dtype)
    if route:
        route_ref[...] = _route_top2(h, rw_ref)


def _out_proj(x, sb, df, wo, g, router_w=None):
    n = x.shape[0]
    route = router_w is not None
    rows = lambda w: pl.BlockSpec((ROW_TILE, w), lambda i: (i, 0))
    whole = lambda a: pl.BlockSpec(a.shape, lambda i: (0, 0))
    in_specs = [rows(D_MODEL), rows(SB_WIDTH), rows(DIFF_WIDTH), whole(wo), whole(g)]
    out_specs = [rows(D_MODEL), rows(D_MODEL)]
    out_shape = [jax.ShapeDtypeStruct((n, D_MODEL), F32),
                 jax.ShapeDtypeStruct((n, D_MODEL), F32 if route else BF16)]
    args = [x, sb, df, wo, g]
    if route:
        in_specs.append(whole(router_w))
        out_specs.append(rows(LANES))
        out_shape.append(jax.ShapeDtypeStruct((n, LANES), F32))
        args.append(router_w)
    return pl.pallas_call(
        functools.partial(_out_proj_kernel, route=route),
        grid=(n // ROW_TILE,),
        in_specs=in_specs,
        out_specs=out_specs,
        out_shape=out_shape,
        compiler_params=_params("parallel"),
        name="out_proj_route" if route else "out_proj",
    )(*args)


def _dense_ffn_kernel(x_ref, h_ref, wg_ref, wu_ref, wd_ref, o_ref):
    h = h_ref[...]
    gate = jnp.dot(h, wg_ref[...], preferred_element_type=F32)
    up = jnp.dot(h, wu_ref[...], preferred_element_type=F32)
    act = (gate * jax.nn.sigmoid(gate) * up).astype(BF16)
    o_ref[...] = x_ref[...] + jnp.dot(act, wd_ref[...], preferred_element_type=F32)


def _dense_ffn(x, h, wg, wu, wd):
    n = x.shape[0]
    tm = ROW_TILE // 2
    rows = pl.BlockSpec((tm, D_MODEL), lambda i: (i, 0))
    whole = lambda a: pl.BlockSpec(a.shape, lambda i: (0, 0), pipeline_mode=pl.Buffered(1))
    return pl.pallas_call(
        _dense_ffn_kernel,
        grid=(n // tm,),
        in_specs=[rows, rows, whole(wg), whole(wu), whole(wd)],
        out_specs=rows,
        out_shape=jax.ShapeDtypeStruct((n, D_MODEL), F32),
        compiler_params=_params("parallel"),
        name="dense_ffn",
    )(x, h, wg, wu, wd)


def _expert_kernel(be_ref, src_ref, dst_ref, h_ref, wg_ref, wu_ref, wd_ref, y_ref,
                   xbuf, xb_ref, obuf, acc_ref, gsem, ssem):
    tm = MOE_TILE
    i = pl.program_id(0)
    f = pl.program_id(1)
    nb = pl.num_programs(0)
    nf = pl.num_programs(1)
    chunk = tm // MOE_FF_STEPS

    def gather_row(r, row):
        return pltpu.make_async_copy(h_ref.at[pl.ds(row, 1), :], xbuf.at[pl.ds(r, 1), :], gsem)

    def scatter_row(r, row):
        return pltpu.make_async_copy(obuf.at[pl.ds(r, 1), :], y_ref.at[pl.ds(row, 1), :], ssem)

    def wait_gather():
        pltpu.make_async_copy(h_ref.at[pl.ds(0, tm), :], xbuf, gsem).wait()

    def wait_scatter():
        pltpu.make_async_copy(obuf, y_ref.at[pl.ds(0, tm), :], ssem).wait()

    @pl.when((i == 0) & (f == 0))
    def _():
        obuf[...] = jnp.zeros_like(obuf)

        def first(r, carry):
            gather_row(r, src_ref[r]).start()
            return carry

        lax.fori_loop(0, tm, first, 0)

    @pl.when(f == 0)
    def _():
        wait_gather()
        xb_ref[...] = xbuf[...].astype(BF16)
        acc_ref[...] = jnp.zeros_like(acc_ref)

    for r in range(chunk):
        rr = f * chunk + r
        gather_row(rr, src_ref[(i + 1) * tm + rr]).start()
        scatter_row(rr, dst_ref[i * tm + rr]).start()

    x = xb_ref[...]
    gate = jnp.dot(x, wg_ref[...], preferred_element_type=F32)
    up = jnp.dot(x, wu_ref[...], preferred_element_type=F32)
    act = (gate * jax.nn.sigmoid(gate) * up).astype(BF16)
    acc_ref[...] += jnp.dot(act, wd_ref[...], preferred_element_type=F32)

    @pl.when(f == nf - 1)
    def _():
        wait_scatter()
        obuf[...] = acc_ref[...]

        @pl.when(i == nb - 1)
        def _():
            def last(r, carry):
                scatter_row(r, dst_ref[(i + 1) * tm + r]).start()
                return carry

            lax.fori_loop(0, tm, last, 0)
            wait_scatter()
            wait_gather()


def _expert_ffn(h, block_expert, src_rows, dst_rows, wg, wu, wd):
    n_slots = src_rows.shape[0]
    nb = n_slots // MOE_TILE - 1
    d_ff = wg.shape[-1]
    nf = d_ff // MOE_FF_TILE
    assert nf == MOE_FF_STEPS and MOE_TILE % nf == 0
    return pl.pallas_call(
        _expert_kernel,
        grid_spec=pltpu.PrefetchScalarGridSpec(
            num_scalar_prefetch=3,
            grid=(nb, nf),
            in_specs=[
                pl.BlockSpec(memory_space=pl.ANY),
                pl.BlockSpec((None, D_MODEL, MOE_FF_TILE), lambda i, f, be, s, d: (be[i], 0, f)),
                pl.BlockSpec((None, D_MODEL, MOE_FF_TILE), lambda i, f, be, s, d: (be[i], 0, f)),
                pl.BlockSpec((None, MOE_FF_TILE, D_MODEL), lambda i, f, be, s, d: (be[i], f, 0)),
            ],
            out_specs=pl.BlockSpec(memory_space=pl.ANY),
            scratch_shapes=[pltpu.VMEM((MOE_TILE, D_MODEL), F32),
                            pltpu.VMEM((MOE_TILE, D_MODEL), BF16),
                            pltpu.VMEM((MOE_TILE, D_MODEL), F32),
                            pltpu.VMEM((MOE_TILE, D_MODEL), F32),
                            pltpu.SemaphoreType.DMA(()), pltpu.SemaphoreType.DMA(())],
        ),
        out_shape=jax.ShapeDtypeStruct((n_slots, D_MODEL), F32),
        compiler_params=_params("arbitrary", "arbitrary"),
        name="expert_ffn",
    )(block_expert, src_rows, dst_rows, h, wg, wu, wd)


def _combine_kernel(x_ref, y0_ref, y1_ref, route_ref, g_ref, o_ref, *, final):
    r = route_ref[...]
    out = x_ref[...] + r[:, 2:3] * y0_ref[...] + r[:, 3:4] * y1_ref[...]
    if final:
        out = _rms(out, g_ref[...])
    o_ref[...] = out


def _combine(x, y, route, g, final):
    n = x.shape[0]
    second = n // ROW_TILE
    rows = lambda w: pl.BlockSpec((ROW_TILE, w), lambda i: (i, 0))
    return pl.pallas_call(
        functools.partial(_combine_kernel, final=final),
        grid=(n // ROW_TILE,),
        in_specs=[rows(D_MODEL), rows(D_MODEL),
                  pl.BlockSpec((ROW_TILE, D_MODEL), lambda i: (second + i, 0)),
                  rows(LANES), pl.BlockSpec((1, D_MODEL), lambda i: (0, 0))],
        out_specs=rows(D_MODEL),
        out_shape=jax.ShapeDtypeStruct((n, D_MODEL), F32),
        compiler_params=_params("parallel"),
        name="moe_combine",
    )(x, y, y, route, g)


def _moe(x, h, route, wg, wu, wd, final_g, final):
    n = x.shape[0]
    n_assign = n * TOP_K
    n_blocks = n_assign // MOE_TILE + N_EXPERTS
    cap = n_blocks * MOE_TILE
    experts = route[:, :TOP_K].astype(jnp.int32).T.reshape(n_assign)
    onehot = (experts[:, None] == jnp.arange(N_EXPERTS, dtype=jnp.int32)[None, :]).astype(jnp.int32)
    running = jnp.cumsum(onehot, axis=0)
    rank = jnp.sum(running * onehot, axis=1) - 1
    counts = running[-1]
    padded = (counts + MOE_TILE - 1) // MOE_TILE * MOE_TILE
    pad_ends = jnp.cumsum(padded)
    pad_starts = pad_ends - padded
    slot_of = jnp.sum(onehot * pad_starts[None, :], axis=1) + rank
    assign_of = jnp.full((cap,), -1, jnp.int32).at[slot_of].set(
        jnp.arange(n_assign, dtype=jnp.int32))
    empty = assign_of < 0
    spare = n_assign + jnp.cumsum(empty.astype(jnp.int32)) - 1
    dst = jnp.where(empty, spare, assign_of)
    src = jnp.where(empty, 0, assign_of % n)
    lead = cap + jnp.arange(MOE_TILE, dtype=jnp.int32)
    dst_rows = jnp.concatenate([lead, dst]).astype(jnp.int32)
    src_rows = jnp.concatenate([src, jnp.zeros((MOE_TILE,), jnp.int32)]).astype(jnp.int32)
    block_start = jnp.arange(n_blocks, dtype=jnp.int32) * MOE_TILE
    block_expert = jnp.minimum(
        jnp.sum((block_start[:, None] >= pad_ends[None, :]).astype(jnp.int32), axis=1),
        N_EXPERTS - 1).astype(jnp.int32)

    y = _expert_ffn(h, block_expert, src_rows, dst_rows, wg, wu, wd)
    return _combine(x, y, route, final_g, final)


def _final_norm_kernel(x_ref, g_ref, o_ref):
    o_ref[...] = _rms(x_ref[...], g_ref[...])


def _final_norm(x, g):
    n = x.shape[0]
    rows = pl.BlockSpec((ROW_TILE, D_MODEL), lambda i: (i, 0))
    return pl.pallas_call(
        _final_norm_kernel,
        grid=(n // ROW_TILE,),
        in_specs=[rows, pl.BlockSpec((1, D_MODEL), lambda i: (0, 0))],
        out_specs=rows,
        out_shape=jax.ShapeDtypeStruct((n, D_MODEL), F32),
        compiler_params=_params("parallel"),
        name="final_norm",
    )(x, g)


def kernel(x, w_in, w_out, attn_norm, ffn_norm, sb_out_norm, diff_subln, lambda_q1, lambda_k1,
           lambda_q2, lambda_k2, rel_bias, dense_w_gate, dense_w_up, dense_w_down, router_w,
           expert_w_gate, expert_w_up, expert_w_down, final_norm):
    b, s, d = x.shape
    depth = w_in.shape[0]
    n = b * s
    assert d == D_MODEL and s % ATT_TILE == 0 and s % ROW_TILE == 0
    assert (n * TOP_K) % MOE_TILE == 0

    scale = HEAD_DIM ** -0.5
    sb_q, sb_k, sb_v = 0, SB_WIDTH, 2 * SB_WIDTH
    df_q, df_k, df_v = 3 * SB_WIDTH, 3 * SB_WIDTH + DIFF_WIDTH, 3 * SB_WIDTH + 2 * DIFF_WIDTH
    bias_tiles = _bias_tiles(rel_bias)
    row2 = lambda v: v.astype(F32).reshape(1, -1)

    xf = x.reshape(n, d)
    for i in range(depth):
        last = i == depth - 1
        w = w_in[i]
        wqk = jnp.concatenate([w[:, sb_q:sb_k] * scale, w[:, sb_k:sb_v],
                               w[:, df_q:df_k] * scale, w[:, df_k:df_v]], axis=1).astype(BF16)
        wvt = jnp.concatenate([w[:, sb_v:df_q], w[:, df_v:]], axis=1).T.astype(BF16)
        qk, vt = _norm_proj(xf, row2(attn_norm[i]), wqk, wvt, b, s)
        qk = qk.reshape(b, s, QK_WIDTH)
        sb = _sb_attention(qk, vt, row2(jnp.tile(sb_out_norm[i], 2)))
        lambda_init = 0.8 - 0.6 * math.exp(-0.3 * i)
        lam_params = jnp.stack([lambda_q1[i], lambda_k1[i], lambda_q2[i], lambda_k2[i]]).astype(F32)
        df = _diff_attention(qk, vt, bias_tiles, lam_params, row2(diff_subln[i]), lambda_init)
        sb = sb.reshape(n, SB_WIDTH)
        df = df.reshape(n, DIFF_WIDTH)
        wo = w_out[i].astype(BF16)
        j = i // 2
        if i % 2 == 0:
            xf, h = _out_proj(xf, sb, df, wo, row2(ffn_norm[i]))
            xf = _dense_ffn(xf, h, dense_w_gate[j].astype(BF16), dense_w_up[j].astype(BF16),
                            dense_w_down[j].astype(BF16))
            if last:
                xf = _final_norm(xf, row2(final_norm))
        else:
            rw = jnp.pad(router_w[j].astype(F32), ((0, 0), (0, LANES - N_EXPERTS)))
            rw_hi = rw.astype(BF16)
            rw_lo = (rw - rw_hi.astype(F32)).astype(BF16)
            xf, h, route = _out_proj(xf, sb, df, wo, row2(ffn_norm[i]),
                                     jnp.concatenate([rw_hi, rw_lo], axis=1))
            xf = _moe(xf, h, route, expert_w_gate[j].astype(BF16), expert_w_up[j].astype(BF16),
                      expert_w_down[j].astype(BF16), row2(final_norm), last)
    return xf.reshape(b, s, d)
```

```python
import functools
import math

import jax
import jax.numpy as jnp
from jax import lax
from jax.experimental import pallas as pl
from jax.experimental.pallas import tpu as pltpu

D_MODEL = 1024
HEAD_DIM = 64
SB_HEADS = 8
DIFF_HEADS = 4
SB_WIDTH = SB_HEADS * HEAD_DIM
DIFF_WIDTH = DIFF_HEADS * 2 * HEAD_DIM
QK_WIDTH = 2 * SB_WIDTH + 2 * DIFF_WIDTH
V_WIDTH = SB_WIDTH + DIFF_WIDTH
N_BUCKETS = 32
MAX_DISTANCE = 128
N_EXPERTS = 8
TOP_K = 2
NORM_EPS = 1e-6

LANES = 128
MXU_WIDTH = 256
SPARE_ROWS = 8
VMEM_LIMIT = 56 * 1024 * 1024

ROW_TILE = 512
ATT_TILE = 256
MOE_TILE = 512
MOE_FF_TILE = 1792
MOE_FF_STEPS = 2
SB_LOG_WEIGHT_FLOOR = -150.0
MASKED_LOGIT = -1e30
LOG2_E = math.log2(math.e)
DIFF_SUM_ROWS = 16

F32 = jnp.float32
BF16 = jnp.bfloat16
NT_DIMS = (((1,), (1,)), ((), ()))


def _params(*semantics):
    return pltpu.CompilerParams(dimension_semantics=semantics, vmem_limit_bytes=VMEM_LIMIT)


def _rms(x, g):
    return x * lax.rsqrt(jnp.mean(x * x, axis=-1, keepdims=True) + NORM_EPS) * g


def _norm_proj_kernel(x_ref, g_ref, wqk_ref, wvt_ref, qk_ref, vt_ref):
    h = _rms(x_ref[...], g_ref[...]).astype(BF16)
    for c in range(QK_WIDTH // D_MODEL):
        cols = slice(c * D_MODEL, (c + 1) * D_MODEL)
        qk_ref[:, cols] = jnp.dot(h, wqk_ref[:, cols], preferred_element_type=F32).astype(BF16)
    vt_ref[...] = lax.dot_general(wvt_ref[...], h, NT_DIMS,
                                  preferred_element_type=F32).astype(BF16)


def _norm_proj(x, g, wqk, wvt, batch, seq):
    n = x.shape[0]
    per_seq = seq // ROW_TILE
    whole = lambda a: pl.BlockSpec(a.shape, lambda b, i: (0, 0), pipeline_mode=pl.Buffered(1))
    return pl.pallas_call(
        _norm_proj_kernel,
        grid=(batch, per_seq),
        in_specs=[
            pl.BlockSpec((ROW_TILE, D_MODEL), lambda b, i: (b * per_seq + i, 0)),
            pl.BlockSpec((1, D_MODEL), lambda b, i: (0, 0)),
            whole(wqk), whole(wvt),
        ],
        out_specs=[
            pl.BlockSpec((ROW_TILE, QK_WIDTH), lambda b, i: (b * per_seq + i, 0)),
            pl.BlockSpec((None, V_WIDTH, ROW_TILE), lambda b, i: (b, 0, i)),
        ],
        out_shape=[jax.ShapeDtypeStruct((n, QK_WIDTH), BF16),
                   jax.ShapeDtypeStruct((batch, V_WIDTH, seq), BF16)],
        compiler_params=_params("parallel", "parallel"),
        name="norm_proj",
    )(x, g, wqk, wvt)


def _sb_kernel(q_ref, k_ref, vt_ref, g_ref, o_ref, qq_ref, acc_ref):
    t = ATT_TILE
    i = pl.program_id(2)
    q = q_ref[...]
    lane = lax.broadcasted_iota(jnp.int32, (1, LANES), 1)
    qq_ref[0:t, :] = jnp.where(lane < HEAD_DIM, q, jnp.zeros_like(q))
    qq_ref[t:2 * t, :] = jnp.where(lane < HEAD_DIM, jnp.zeros_like(q), q)
    key = lax.broadcasted_iota(jnp.int32, (t, t), 0)
    pos = lax.broadcasted_iota(jnp.int32, (t, t), 1)
    suffix_ones = (pos >= key).astype(BF16)
    key2 = lax.broadcasted_iota(jnp.int32, (t, 2 * t), 0)
    qry2 = lax.broadcasted_iota(jnp.int32, (t, 2 * t), 1)
    strictly_causal = key2 < jnp.where(qry2 >= t, qry2 - t, qry2)

    def scores(j):
        start = pl.multiple_of(j * t, t)
        return lax.dot_general(k_ref[pl.ds(start, t), :], qq_ref[...], NT_DIMS,
                               preferred_element_type=F32)

    def split_log_stay(z, diagonal):
        neg_z = -z
        log_stay = jnp.minimum(neg_z, 0.0) - jnp.log2(1.0 + jnp.exp2(jnp.minimum(z, neg_z)))
        if diagonal:
            log_stay = jnp.where(strictly_causal, log_stay, 0.0)
        hi = log_stay.astype(BF16)
        return hi, (log_stay - hi.astype(F32)).astype(BF16)

    def suffix_sums(hi, lo):
        return (jnp.dot(suffix_ones, hi, preferred_element_type=F32)
                + jnp.dot(suffix_ones, lo, preferred_element_type=F32))

    def weights(z, incl, carry, diagonal):
        log_a = z + incl
        if carry is not None:
            log_a = log_a + carry
        a = jnp.exp2(log_a)
        if diagonal:
            a = jnp.where(strictly_causal, a, 0.0)
        return a.astype(BF16), incl[0:1, :] if carry is None else carry + incl[0:1, :]

    def accumulate(j, a, first):
        start = pl.multiple_of(j * t, t)
        update = jnp.dot(vt_ref[:, pl.ds(start, t)], a, preferred_element_type=F32)
        if first:
            acc_ref[...] = update
        else:
            acc_ref[...] += update

    def tile(j, carry, diagonal):
        z = scores(j)
        a, carry = weights(z, suffix_sums(*split_log_stay(z, diagonal)), carry, diagonal)
        accumulate(j, a, diagonal)
        return carry

    @pl.when(i == 0)
    def _():
        tile(i, None, True)

    @pl.when(i > 0)
    def _():
        z_d = scores(i)
        z_p = scores(i - 1)
        split_d = split_log_stay(z_d, True)
        incl_d = suffix_sums(*split_d)
        split_p = split_log_stay(z_p, False)
        a_d, carry = weights(z_d, incl_d, None, True)
        incl_p = suffix_sums(*split_p)
        accumulate(i, a_d, True)
        a_p, carry = weights(z_p, incl_p, carry, False)
        accumulate(i - 1, a_p, False)

        def cond(state):
            j, _, top_carry = state
            return (j >= 0) & (top_carry > SB_LOG_WEIGHT_FLOOR)

        def body(state):
            j, carry, _ = state
            carry = tile(j, carry, False)
            return j - 1, carry, jnp.max(carry)

        lax.while_loop(cond, body, (i - 2, carry, jnp.max(carry)))

    normed = []
    for h in range(2):
        o = acc_ref[h * HEAD_DIM:(h + 1) * HEAD_DIM, h * t:(h + 1) * t]
        ms = jnp.mean(o * o, axis=0, keepdims=True)
        normed.append(o * lax.rsqrt(ms + NORM_EPS))
    o_ref[...] = (jnp.concatenate(normed, axis=0).T * g_ref[...]).astype(o_ref.dtype)


def _sb_attention(qk, vt, g2):
    b, s, _ = qk.shape
    t = ATT_TILE
    pairs = SB_WIDTH // LANES
    return pl.pallas_call(
        _sb_kernel,
        grid=(b, pairs, s // t),
        in_specs=[
            pl.BlockSpec((None, t, LANES), lambda bi, p, i: (bi, i, p)),
            pl.BlockSpec((None, s, LANES), lambda bi, p, i: (bi, 0, pairs + p)),
            pl.BlockSpec((None, LANES, s), lambda bi, p, i: (bi, p, 0)),
            pl.BlockSpec((1, LANES), lambda bi, p, i: (0, 0)),
        ],
        out_specs=pl.BlockSpec((None, t, LANES), lambda bi, p, i: (bi, i, p)),
        out_shape=jax.ShapeDtypeStruct((b, s, SB_WIDTH), BF16),
        scratch_shapes=[pltpu.VMEM((2 * t, LANES), BF16), pltpu.VMEM((LANES, 2 * t), F32)],
        compiler_params=_params("parallel", "parallel", "arbitrary"),
        name="sb_attention",
    )(qk, qk, vt, g2)


def _diff_kernel(q_ref, k_ref, vt_ref, bias_ref, lam_ref, g_ref, o_ref,
                 qq_ref, z0_ref, z1_ref, p0_ref, p1_ref, a0_ref, a1_ref, m_ref, acc_ref,
                 *, lambda_init):
    t = ATT_TILE
    i = pl.program_id(2)
    q = q_ref[...]
    lane = lax.broadcasted_iota(jnp.int32, (1, LANES), 1)
    qq_ref[0:t, :] = jnp.where(lane < HEAD_DIM, q, jnp.zeros_like(q))
    qq_ref[t:2 * t, :] = jnp.where(lane < HEAD_DIM, jnp.zeros_like(q), q)
    ones_rows = jnp.ones((DIFF_SUM_ROWS, t), BF16)

    def scores(n, z_ref):
        start = pl.multiple_of(jnp.maximum(i - n, 0) * t, t)
        kind = jnp.where(n > i, 3, jnp.minimum(n, 2))
        z_ref[...] = lax.dot_general(k_ref[pl.ds(start, t), :], qq_ref[...], NT_DIMS,
                                     preferred_element_type=F32) + bias_ref[kind]

    def softmax(z_ref, p_ref, a_ref):
        z = z_ref[...]
        m_prev = m_ref[...]
        m_new = jnp.maximum(m_prev, jnp.max(z, axis=0, keepdims=True))
        a_ref[...] = jnp.exp2(m_prev - m_new)
        p_ref[...] = jnp.exp2(z - m_new).astype(BF16)
        m_ref[...] = m_new

    def values(n, p_ref, a_ref):
        start = pl.multiple_of(jnp.maximum(i - n, 0) * t, t)
        vt = jnp.concatenate([vt_ref[:, pl.ds(start, t)], ones_rows], axis=0)
        acc_ref[...] = a_ref[...] * acc_ref[...] + jnp.dot(vt, p_ref[...],
                                                           preferred_element_type=F32)

    m_ref[...] = jnp.full_like(m_ref, MASKED_LOGIT)
    acc_ref[...] = jnp.zeros_like(acc_ref)
    scores(0, z0_ref)
    scores(1, z1_ref)
    softmax(z0_ref, p0_ref, a0_ref)

    def pair(k, carry):
        n = 2 * k + 2
        scores(n, z0_ref)
        softmax(z1_ref, p1_ref, a1_ref)
        values(n - 2, p0_ref, a0_ref)
        scores(n + 1, z1_ref)
        softmax(z0_ref, p0_ref, a0_ref)
        values(n - 1, p1_ref, a1_ref)
        return carry

    lax.fori_loop(0, (i + 2) // 2, pair, 0)

    lp = lam_ref[...]
    lam = (jnp.exp(jnp.sum(lp[0:1] * lp[1:2], axis=-1, keepdims=True))
           - jnp.exp(jnp.sum(lp[2:3] * lp[3:4], axis=-1, keepdims=True)) + lambda_init)
    normalised = acc_ref[0:LANES, :] / acc_ref[LANES:LANES + 1, :]
    o = normalised[:, :t] - lam * normalised[:, t:]
    o = o * lax.rsqrt(jnp.mean(o * o, axis=0, keepdims=True) + NORM_EPS)
    o_ref[...] = (o.T * g_ref[...] * (1.0 - lambda_init)).astype(o_ref.dtype)


def _diff_attention(qk, vt, bias_tiles, lam_params, g, lambda_init):
    b, s, _ = qk.shape
    t = ATT_TILE
    q0 = 2 * SB_WIDTH // LANES
    v0 = SB_WIDTH // LANES
    tile_f32 = pltpu.VMEM((t, 2 * t), F32)
    tile_bf16 = pltpu.VMEM((t, 2 * t), BF16)
    row_f32 = pltpu.VMEM((1, 2 * t), F32)
    return pl.pallas_call(
        functools.partial(_diff_kernel, lambda_init=lambda_init),
        grid=(b, DIFF_HEADS, s // t),
        in_specs=[
            pl.BlockSpec((None, t, LANES), lambda bi, h, i: (bi, i, q0 + h)),
            pl.BlockSpec((None, s, LANES), lambda bi, h, i: (bi, 0, q0 + DIFF_HEADS + h)),
            pl.BlockSpec((None, LANES, s), lambda bi, h, i: (bi, v0 + h, 0)),
            pl.BlockSpec((None, 4, t, 2 * t), lambda bi, h, i: (h, 0, 0, 0)),
            pl.BlockSpec((4, HEAD_DIM), lambda bi, h, i: (0, 0)),
            pl.BlockSpec((1, LANES), lambda bi, h, i: (0, 0)),
        ],
        out_specs=pl.BlockSpec((None, t, LANES), lambda bi, h, i: (bi, i, h)),
        out_shape=jax.ShapeDtypeStruct((b, s, DIFF_WIDTH), BF16),
        scratch_shapes=[pltpu.VMEM((2 * t, LANES), BF16), tile_f32, tile_f32, tile_bf16, tile_bf16,
                        row_f32, row_f32, row_f32,
                        pltpu.VMEM((LANES + DIFF_SUM_ROWS, 2 * t), F32)],
        compiler_params=_params("parallel", "parallel", "arbitrary"),
        name="diff_attention",
    )(qk, qk, vt, bias_tiles, lam_params, g)


def _t5_causal_bucket(n):
    max_exact = N_BUCKETS // 2
    nf = jnp.maximum(n, 1).astype(F32)
    large = max_exact + (jnp.log(nf / max_exact) / math.log(MAX_DISTANCE / max_exact)
                         * (N_BUCKETS - max_exact)).astype(jnp.int32)
    large = jnp.minimum(large, N_BUCKETS - 1)
    return jnp.where(n < max_exact, n, large)


def _bias_tiles(rel_bias):
    t = ATT_TILE
    assert MAX_DISTANCE <= t
    key = jnp.arange(t, dtype=jnp.int32)[:, None]
    qry = jnp.arange(t, dtype=jnp.int32)[None, :]
    dist = jnp.stack([jnp.maximum(qry - key, 0), qry - key + t])
    bucket = _t5_causal_bucket(dist)
    table = (rel_bias.astype(F32) - rel_bias.astype(F32)[N_BUCKETS - 1][None, :]) * LOG2_E
    near = jnp.zeros((DIFF_HEADS,) + dist.shape, F32)
    for k in range(N_BUCKETS - 1):
        near = jnp.where(bucket[None] == k, table[k][:, None, None, None], near)
    diag = jnp.where((key <= qry)[None], near[:, 0], MASKED_LOGIT)
    tiles = jnp.stack([diag, near[:, 1], jnp.zeros_like(diag), jnp.full_like(diag, MASKED_LOGIT)],
                      axis=1)
    return jnp.concatenate([tiles, tiles], axis=-1)


def _route_top2(h, rw_ref):
    h_hi = h.astype(BF16)
    h_lo = (h - h_hi.astype(F32)).astype(BF16)
    both = jnp.dot(h_hi, rw_ref[...], preferred_element_type=F32)
    logits = (both[:, :LANES] + both[:, LANES:]
              + jnp.dot(h_lo, rw_ref[:, :LANES], preferred_element_type=F32))
    lane = lax.broadcasted_iota(jnp.int32, logits.shape, 1)
    neg = -jnp.inf
    lg = jnp.where(lane < N_EXPERTS, logits, neg)
    m1 = jnp.max(lg, axis=-1, keepdims=True)
    i1 = jnp.min(jnp.where(lg == m1, lane, LANES), axis=-1, keepdims=True)
    lg2 = jnp.where(lane == i1, neg, lg)
    m2 = jnp.max(lg2, axis=-1, keepdims=True)
    i2 = jnp.min(jnp.where(lg2 == m2, lane, LANES), axis=-1, keepdims=True)
    e = jnp.exp(m2 - m1)
    g1 = 1.0 / (1.0 + e)
    g2 = e * g1
    return jnp.where(lane == 0, i1.astype(F32),
                     jnp.where(lane == 1, i2.astype(F32),
                               jnp.where(lane == 2, g1, jnp.where(lane == 3, g2, 0.0))))


def _out_proj_kernel(x_ref, sb_ref, df_ref, wo_ref, g_ref, *rest, route):
    if route:
        rw_ref, xo_ref, h_ref, route_ref = rest
    else:
        xo_ref, h_ref = rest
    y = (jnp.dot(sb_ref[...], wo_ref[:SB_WIDTH, :], preferred_element_type=F32)
         + jnp.dot(df_ref[...], wo_ref[SB_WIDTH:, :], preferred_element_type=F32))
    xn = x_ref[...] + y
    xo_ref[...] = xn
    h = _rms(xn, g_ref[...])
    h_ref[...] = h.astype(h_ref.dtype)
    if route:
        route_ref[...] = _route_top2(h, rw_ref)


def _out_proj(x, sb, df, wo, g, router_w=None):
    n = x.shape[0]
    route = router_w is not None
    rows = lambda w: pl.BlockSpec((ROW_TILE, w), lambda i: (i, 0))
    whole = lambda a: pl.BlockSpec(a.shape, lambda i: (0, 0))
    in_specs = [rows(D_MODEL), rows(SB_WIDTH), rows(DIFF_WIDTH), whole(wo), whole(g)]
    out_specs = [rows(D_MODEL), rows(D_MODEL)]
    out_shape = [jax.ShapeDtypeStruct((n, D_MODEL), F32),
                 jax.ShapeDtypeStruct((n, D_MODEL), F32 if route else BF16)]
    args = [x, sb, df, wo, g]
    if route:
        in_specs.append(whole(router_w))
        out_specs.append(rows(LANES))
        out_shape.append(jax.ShapeDtypeStruct((n, LANES), F32))
        args.append(router_w)
    return pl.pallas_call(
        functools.partial(_out_proj_kernel, route=route),
        grid=(n // ROW_TILE,),
        in_specs=in_specs,
        out_specs=out_specs,
        out_shape=out_shape,
        compiler_params=_params("parallel"),
        name="out_proj_route" if route else "out_proj",
    )(*args)


def _dense_ffn_kernel(x_ref, h_ref, wg_ref, wu_ref, wd_ref, o_ref):
    h = h_ref[...]
    gate = jnp.dot(h, wg_ref[...], preferred_element_type=F32)
    up = jnp.dot(h, wu_ref[...], preferred_element_type=F32)
    act = (gate * jax.nn.sigmoid(gate) * up).astype(BF16)
    o_ref[...] = x_ref[...] + jnp.dot(act, wd_ref[...], preferred_element_type=F32)


def _dense_ffn(x, h, wg, wu, wd):
    n = x.shape[0]
    tm = ROW_TILE // 2
    rows = pl.BlockSpec((tm, D_MODEL), lambda i: (i, 0))
    whole = lambda a: pl.BlockSpec(a.shape, lambda i: (0, 0), pipeline_mode=pl.Buffered(1))
    return pl.pallas_call(
        _dense_ffn_kernel,
        grid=(n // tm,),
        in_specs=[rows, rows, whole(wg), whole(wu), whole(wd)],
        out_specs=rows,
        out_shape=jax.ShapeDtypeStruct((n, D_MODEL), F32),
        compiler_params=_params("parallel"),
        name="dense_ffn",
    )(x, h, wg, wu, wd)


def _expert_kernel(be_ref, src_ref, dst_ref, h_ref, wg_ref, wu_ref, wd_ref, y_ref,
                   xbuf, xb_ref, obuf, acc_ref, gsem, ssem):
    tm = MOE_TILE
    i = pl.program_id(0)
    f = pl.program_id(1)
    nb = pl.num_programs(0)
    nf = pl.num_programs(1)
    chunk = tm // MOE_FF_STEPS

    def gather_row(r, row):
        return pltpu.make_async_copy(h_ref.at[pl.ds(row, 1), :], xbuf.at[pl.ds(r, 1), :], gsem)

    def scatter_row(r, row):
        return pltpu.make_async_copy(obuf.at[pl.ds(r, 1), :], y_ref.at[pl.ds(row, 1), :], ssem)

    def wait_gather():
        pltpu.make_async_copy(h_ref.at[pl.ds(0, tm), :], xbuf.at[pl.ds(0, tm), :], gsem).wait()

    def wait_scatter():
        pltpu.make_async_copy(obuf, y_ref.at[pl.ds(0, tm), :], ssem).wait()

    @pl.when((i == 0) & (f == 0))
    def _():
        obuf[...] = jnp.zeros_like(obuf)
        xbuf[tm:, :] = jnp.zeros((SPARE_ROWS, D_MODEL), F32)

        def first(r, carry):
            gather_row(r, src_ref[r]).start()
            return carry

        lax.fori_loop(0, tm, first, 0)

    @pl.when(f == 0)
    def _():
        wait_gather()
        xb_ref[...] = xbuf[0:tm, :].astype(BF16)
        acc_ref[...] = jnp.zeros_like(acc_ref)

    x = xb_ref[...]
    n_slices = MOE_FF_TILE // MXU_WIDTH
    done = 0
    anchor = None
    for c in range(n_slices):
        cols = slice(c * MXU_WIDTH, (c + 1) * MXU_WIDTH)
        gate = jnp.dot(x, wg_ref[:, cols], preferred_element_type=F32)
        up = jnp.dot(x, wu_ref[:, cols], preferred_element_type=F32)
        if anchor is not None:
            up = up + anchor
        act = (gate * jax.nn.sigmoid(gate) * up).astype(BF16)
        acc_ref[...] += jnp.dot(act, wd_ref[cols, :], preferred_element_type=F32)
        upto = chunk * (c + 1) // n_slices
        for r in range(done, upto):
            rr = f * chunk + r
            gather_row(rr, src_ref[(i + 1) * tm + rr]).start()
            scatter_row(rr, dst_ref[i * tm + rr]).start()
        done = upto
        anchor = jnp.where(i < 0, xbuf[tm:tm + 1, 0:MXU_WIDTH], 0.0)

    @pl.when(f == nf - 1)
    def _():
        wait_scatter()
        obuf[...] = acc_ref[...]

        @pl.when(i == nb - 1)
        def _():
            def last(r, carry):
                scatter_row(r, dst_ref[(i + 1) * tm + r]).start()
                return carry

            lax.fori_loop(0, tm, last, 0)
            wait_scatter()
            wait_gather()


def _expert_ffn(h, block_expert, src_rows, dst_rows, wg, wu, wd):
    n_slots = src_rows.shape[0]
    nb = n_slots // MOE_TILE - 1
    d_ff = wg.shape[-1]
    nf = d_ff // MOE_FF_TILE
    assert nf == MOE_FF_STEPS and MOE_TILE % nf == 0
    return pl.pallas_call(
        _expert_kernel,
        grid_spec=pltpu.PrefetchScalarGridSpec(
            num_scalar_prefetch=3,
            grid=(nb, nf),
            in_specs=[
                pl.BlockSpec(memory_space=pl.ANY),
                pl.BlockSpec((None, D_MODEL, MOE_FF_TILE), lambda i, f, be, s, d: (be[i], 0, f)),
                pl.BlockSpec((None, D_MODEL, MOE_FF_TILE), lambda i, f, be, s, d: (be[i], 0, f)),
                pl.BlockSpec((None, MOE_FF_TILE, D_MODEL), lambda i, f, be, s, d: (be[i], f, 0)),
            ],
            out_specs=pl.BlockSpec(memory_space=pl.ANY),
            scratch_shapes=[pltpu.VMEM((MOE_TILE + SPARE_ROWS, D_MODEL), F32),
                            pltpu.VMEM((MOE_TILE, D_MODEL), BF16),
                            pltpu.VMEM((MOE_TILE, D_MODEL), F32),
                            pltpu.VMEM((MOE_TILE, D_MODEL), F32),
                            pltpu.SemaphoreType.DMA(()), pltpu.SemaphoreType.DMA(())],
        ),
        out_shape=jax.ShapeDtypeStruct((n_slots, D_MODEL), F32),
        compiler_params=_params("arbitrary", "arbitrary"),
        name="expert_ffn",
    )(block_expert, src_rows, dst_rows, h, wg, wu, wd)


def _combine_kernel(x_ref, y0_ref, y1_ref, route_ref, g_ref, o_ref, *, final):
    r = route_ref[...]
    out = x_ref[...] + r[:, 2:3] * y0_ref[...] + r[:, 3:4] * y1_ref[...]
    if final:
        out = _rms(out, g_ref[...])
    o_ref[...] = out


def _combine(x, y, route, g, final):
    n = x.shape[0]
    second = n // ROW_TILE
    rows = lambda w: pl.BlockSpec((ROW_TILE, w), lambda i: (i, 0))
    return pl.pallas_call(
        functools.partial(_combine_kernel, final=final),
        grid=(n // ROW_TILE,),
        in_specs=[rows(D_MODEL), rows(D_MODEL),
                  pl.BlockSpec((ROW_TILE, D_MODEL), lambda i: (second + i, 0)),
                  rows(LANES), pl.BlockSpec((1, D_MODEL), lambda i: (0, 0))],
        out_specs=rows(D_MODEL),
        out_shape=jax.ShapeDtypeStruct((n, D_MODEL), F32),
        compiler_params=_params("parallel"),
        name="moe_combine",
    )(x, y, y, route, g)


def _moe(x, h, route, wg, wu, wd, final_g, final):
    n = x.shape[0]
    n_assign = n * TOP_K
    n_blocks = n_assign // MOE_TILE + N_EXPERTS
    cap = n_blocks * MOE_TILE
    experts = route[:, :TOP_K].astype(jnp.int32).T.reshape(n_assign)
    onehot = (experts[:, None] == jnp.arange(N_EXPERTS, dtype=jnp.int32)[None, :]).astype(jnp.int32)
    running = jnp.cumsum(onehot, axis=0)
    rank = jnp.sum(running * onehot, axis=1) - 1
    counts = running[-1]
    padded = (counts + MOE_TILE - 1) // MOE_TILE * MOE_TILE
    pad_ends = jnp.cumsum(padded)
    pad_starts = pad_ends - padded
    slot_of = jnp.sum(onehot * pad_starts[None, :], axis=1) + rank
    assign_of = jnp.full((cap,), -1, jnp.int32).at[slot_of].set(
        jnp.arange(n_assign, dtype=jnp.int32))
    empty = assign_of < 0
    spare = n_assign + jnp.cumsum(empty.astype(jnp.int32)) - 1
    dst = jnp.where(empty, spare, assign_of)
    src = jnp.where(empty, 0, assign_of % n)
    lead = cap + jnp.arange(MOE_TILE, dtype=jnp.int32)
    dst_rows = jnp.concatenate([lead, dst]).astype(jnp.int32)
    src_rows = jnp.concatenate([src, jnp.zeros((MOE_TILE,), jnp.int32)]).astype(jnp.int32)
    block_start = jnp.arange(n_blocks, dtype=jnp.int32) * MOE_TILE
    block_expert = jnp.minimum(
        jnp.sum((block_start[:, None] >= pad_ends[None, :]).astype(jnp.int32), axis=1),
        N_EXPERTS - 1).astype(jnp.int32)

    y = _expert_ffn(h, block_expert, src_rows, dst_rows, wg, wu, wd)
    return _combine(x, y, route, final_g, final)


def _final_norm_kernel(x_ref, g_ref, o_ref):
    o_ref[...] = _rms(x_ref[...], g_ref[...])


def _final_norm(x, g):
    n = x.shape[0]
    rows = pl.BlockSpec((ROW_TILE, D_MODEL), lambda i: (i, 0))
    return pl.pallas_call(
        _final_norm_kernel,
        grid=(n // ROW_TILE,),
        in_specs=[rows, pl.BlockSpec((1, D_MODEL), lambda i: (0, 0))],
        out_specs=rows,
        out_shape=jax.ShapeDtypeStruct((n, D_MODEL), F32),
        compiler_params=_params("parallel"),
        name="final_norm",
    )(x, g)


def kernel(x, w_in, w_out, attn_norm, ffn_norm, sb_out_norm, diff_subln, lambda_q1, lambda_k1,
           lambda_q2, lambda_k2, rel_bias, dense_w_gate, dense_w_up, dense_w_down, router_w,
           expert_w_gate, expert_w_up, expert_w_down, final_norm):
    b, s, d = x.shape
    depth = w_in.shape[0]
    n = b * s
    assert d == D_MODEL and s % ATT_TILE == 0 and s % ROW_TILE == 0
    assert (n * TOP_K) % MOE_TILE == 0

    scale = HEAD_DIM ** -0.5
    sb_q, sb_k, sb_v = 0, SB_WIDTH, 2 * SB_WIDTH
    df_q, df_k, df_v = 3 * SB_WIDTH, 3 * SB_WIDTH + DIFF_WIDTH, 3 * SB_WIDTH + 2 * DIFF_WIDTH
    bias_tiles = _bias_tiles(rel_bias)
    row2 = lambda v: v.astype(F32).reshape(1, -1)

    xf = x.reshape(n, d)
    for i in range(depth):
        last = i == depth - 1
        w = w_in[i]
        wqk = jnp.concatenate([w[:, sb_q:sb_k] * (scale * LOG2_E), w[:, sb_k:sb_v],
                               w[:, df_q:df_k] * (scale * LOG2_E), w[:, df_k:df_v]],
                              axis=1).astype(BF16)
        wvt = jnp.concatenate([w[:, sb_v:df_q], w[:, df_v:]], axis=1).T.astype(BF16)
        qk, vt = _norm_proj(xf, row2(attn_norm[i]), wqk, wvt, b, s)
        qk = qk.reshape(b, s, QK_WIDTH)
        sb = _sb_attention(qk, vt, row2(jnp.tile(sb_out_norm[i], 2)))
        lambda_init = 0.8 - 0.6 * math.exp(-0.3 * i)
        lam_params = jnp.stack([lambda_q1[i], lambda_k1[i], lambda_q2[i], lambda_k2[i]]).astype(F32)
        df = _diff_attention(qk, vt, bias_tiles, lam_params, row2(diff_subln[i]), lambda_init)
        sb = sb.reshape(n, SB_WIDTH)
        df = df.reshape(n, DIFF_WIDTH)
        wo = w_out[i].astype(BF16)
        j = i // 2
        if i % 2 == 0:
            xf, h = _out_proj(xf, sb, df, wo, row2(ffn_norm[i]))
            xf = _dense_ffn(xf, h, dense_w_gate[j].astype(BF16), dense_w_up[j].astype(BF16),
                            dense_w_down[j].astype(BF16))
            if last:
                xf = _final_norm(xf, row2(final_norm))
        else:
            rw = jnp.pad(router_w[j].astype(F32), ((0, 0), (0, LANES - N_EXPERTS)))
            rw_hi = rw.astype(BF16)
            rw_lo = (rw - rw_hi.astype(F32)).astype(BF16)
            xf, h, route = _out_proj(xf, sb, df, wo, row2(ffn_norm[i]),
                                     jnp.concatenate([rw_hi, rw_lo], axis=1))
            xf = _moe(xf, h, route, expert_w_gate[j].astype(BF16), expert_w_up[j].astype(BF16),
                      expert_w_down[j].astype(BF16), row2(final_norm), last)
    return xf.reshape(b, s, d)
```

```python
import functools
import math

import jax
import jax.numpy as jnp
from jax import lax
from jax.experimental import pallas as pl
from jax.experimental.pallas import tpu as pltpu

D_MODEL = 1024
HEAD_DIM = 64
SB_HEADS = 8
DIFF_HEADS = 4
SB_WIDTH = SB_HEADS * HEAD_DIM
DIFF_WIDTH = DIFF_HEADS * 2 * HEAD_DIM
QK_WIDTH = 2 * SB_WIDTH + 2 * DIFF_WIDTH
V_WIDTH = SB_WIDTH + DIFF_WIDTH
N_BUCKETS = 32
MAX_DISTANCE = 128
N_EXPERTS = 8
TOP_K = 2
NORM_EPS = 1e-6

LANES = 128
MXU_WIDTH = 256
SPARE_ROWS = 8
VMEM_LIMIT = 56 * 1024 * 1024

ROW_TILE = 512
ATT_TILE = 256
DIFF_Q_TILE = 512
DIFF_NEAR_TILES = 3
MOE_TILE = 512
MOE_FF_TILE = 1792
MOE_FF_STEPS = 2
SB_LOG_WEIGHT_FLOOR = -150.0
MASKED_LOGIT = -1e30
LOG2_E = math.log2(math.e)
DIFF_SUM_ROWS = 16

F32 = jnp.float32
BF16 = jnp.bfloat16
NT_DIMS = (((1,), (1,)), ((), ()))


def _params(*semantics):
    return pltpu.CompilerParams(dimension_semantics=semantics, vmem_limit_bytes=VMEM_LIMIT)


def _rms(x, g):
    return x * lax.rsqrt(jnp.mean(x * x, axis=-1, keepdims=True) + NORM_EPS) * g


def _norm_proj_kernel(x_ref, g_ref, wqk_ref, wvt_ref, qk_ref, vt_ref):
    h = _rms(x_ref[...], g_ref[...]).astype(BF16)
    for c in range(QK_WIDTH // D_MODEL):
        cols = slice(c * D_MODEL, (c + 1) * D_MODEL)
        qk_ref[:, cols] = jnp.dot(h, wqk_ref[:, cols], preferred_element_type=F32).astype(BF16)
    vt_ref[...] = lax.dot_general(wvt_ref[...], h, NT_DIMS,
                                  preferred_element_type=F32).astype(BF16)


def _norm_proj(x, g, wqk, wvt, batch, seq):
    n = x.shape[0]
    per_seq = seq // ROW_TILE
    whole = lambda a: pl.BlockSpec(a.shape, lambda b, i: (0, 0), pipeline_mode=pl.Buffered(1))
    return pl.pallas_call(
        _norm_proj_kernel,
        grid=(batch, per_seq),
        in_specs=[
            pl.BlockSpec((ROW_TILE, D_MODEL), lambda b, i: (b * per_seq + i, 0)),
            pl.BlockSpec((1, D_MODEL), lambda b, i: (0, 0)),
            whole(wqk), whole(wvt),
        ],
        out_specs=[
            pl.BlockSpec((ROW_TILE, QK_WIDTH), lambda b, i: (b * per_seq + i, 0)),
            pl.BlockSpec((None, V_WIDTH, ROW_TILE), lambda b, i: (b, 0, i)),
        ],
        out_shape=[jax.ShapeDtypeStruct((n, QK_WIDTH), BF16),
                   jax.ShapeDtypeStruct((batch, V_WIDTH, seq), BF16)],
        compiler_params=_params("parallel", "parallel"),
        name="norm_proj",
    )(x, g, wqk, wvt)


def _sb_kernel(q_ref, k_ref, vt_ref, g_ref, o_ref, qq_ref, acc_ref):
    t = ATT_TILE
    i = pl.program_id(2)
    q = q_ref[...]
    lane = lax.broadcasted_iota(jnp.int32, (1, LANES), 1)
    qq_ref[0:t, :] = jnp.where(lane < HEAD_DIM, q, jnp.zeros_like(q))
    qq_ref[t:2 * t, :] = jnp.where(lane < HEAD_DIM, jnp.zeros_like(q), q)
    key = lax.broadcasted_iota(jnp.int32, (t, t), 0)
    pos = lax.broadcasted_iota(jnp.int32, (t, t), 1)
    suffix_ones = (pos >= key).astype(BF16)
    key2 = lax.broadcasted_iota(jnp.int32, (t, 2 * t), 0)
    qry2 = lax.broadcasted_iota(jnp.int32, (t, 2 * t), 1)
    strictly_causal = key2 < jnp.where(qry2 >= t, qry2 - t, qry2)

    def scores(j):
        start = pl.multiple_of(j * t, t)
        return lax.dot_general(k_ref[pl.ds(start, t), :], qq_ref[...], NT_DIMS,
                               preferred_element_type=F32)

    def split_log_stay(z, diagonal):
        neg_z = -z
        log_stay = jnp.minimum(neg_z, 0.0) - jnp.log2(1.0 + jnp.exp2(jnp.minimum(z, neg_z)))
        if diagonal:
            log_stay = jnp.where(strictly_causal, log_stay, 0.0)
        hi = log_stay.astype(BF16)
        return hi, (log_stay - hi.astype(F32)).astype(BF16)

    def suffix_sums(hi, lo):
        return (jnp.dot(suffix_ones, hi, preferred_element_type=F32)
                + jnp.dot(suffix_ones, lo, preferred_element_type=F32))

    def weights(z, incl, carry, diagonal):
        log_a = z + incl
        if carry is not None:
            log_a = log_a + carry
        a = jnp.exp2(log_a)
        if diagonal:
            a = jnp.where(strictly_causal, a, 0.0)
        return a.astype(BF16), incl[0:1, :] if carry is None else carry + incl[0:1, :]

    def accumulate(j, a, first):
        start = pl.multiple_of(j * t, t)
        update = jnp.dot(vt_ref[:, pl.ds(start, t)], a, preferred_element_type=F32)
        if first:
            acc_ref[...] = update
        else:
            acc_ref[...] += update

    def tile(j, carry, diagonal):
        z = scores(j)
        a, carry = weights(z, suffix_sums(*split_log_stay(z, diagonal)), carry, diagonal)
        accumulate(j, a, diagonal)
        return carry

    @pl.when(i == 0)
    def _():
        tile(i, None, True)

    @pl.when(i > 0)
    def _():
        z_d = scores(i)
        z_p = scores(i - 1)
        split_d = split_log_stay(z_d, True)
        incl_d = suffix_sums(*split_d)
        split_p = split_log_stay(z_p, False)
        a_d, carry = weights(z_d, incl_d, None, True)
        incl_p = suffix_sums(*split_p)
        accumulate(i, a_d, True)
        a_p, carry = weights(z_p, incl_p, carry, False)
        accumulate(i - 1, a_p, False)

        def cond(state):
            j, _, top_carry = state
            return (j >= 0) & (top_carry > SB_LOG_WEIGHT_FLOOR)

        def body(state):
            j, carry, _ = state
            carry = tile(j, carry, False)
            return j - 1, carry, jnp.max(carry)

        lax.while_loop(cond, body, (i - 2, carry, jnp.max(carry)))

    normed = []
    for h in range(2):
        o = acc_ref[h * HEAD_DIM:(h + 1) * HEAD_DIM, h * t:(h + 1) * t]
        ms = jnp.mean(o * o, axis=0, keepdims=True)
        normed.append(o * lax.rsqrt(ms + NORM_EPS))
    o_ref[...] = (jnp.concatenate(normed, axis=0).T * g_ref[...]).astype(o_ref.dtype)


def _sb_attention(qk, vt, g2):
    b, s, _ = qk.shape
    t = ATT_TILE
    pairs = SB_WIDTH // LANES
    return pl.pallas_call(
        _sb_kernel,
        grid=(b, pairs, s // t),
        in_specs=[
            pl.BlockSpec((None, t, LANES), lambda bi, p, i: (bi, i, p)),
            pl.BlockSpec((None, s, LANES), lambda bi, p, i: (bi, 0, pairs + p)),
            pl.BlockSpec((None, LANES, s), lambda bi, p, i: (bi, p, 0)),
            pl.BlockSpec((1, LANES), lambda bi, p, i: (0, 0)),
        ],
        out_specs=pl.BlockSpec((None, t, LANES), lambda bi, p, i: (bi, i, p)),
        out_shape=jax.ShapeDtypeStruct((b, s, SB_WIDTH), BF16),
        scratch_shapes=[pltpu.VMEM((2 * t, LANES), BF16), pltpu.VMEM((LANES, 2 * t), F32)],
        compiler_params=_params("parallel", "parallel", "arbitrary"),
        name="sb_attention",
    )(qk, qk, vt, g2)


def _diff_kernel(q_ref, k_ref, vt_ref, bias_ref, lam_ref, g_ref, o_ref,
                 qq_ref, z0_ref, z1_ref, zmax0_ref, zmax1_ref, p0_ref, p1_ref, a0_ref, a1_ref,
                 m_ref, acc_ref, *, lambda_init):
    t = ATT_TILE
    tq = DIFF_Q_TILE
    i = pl.program_id(2)
    n_tiles = (tq // t) * (i + 1)
    q = q_ref[...]
    lane = lax.broadcasted_iota(jnp.int32, (1, LANES), 1)
    qq_ref[0:tq, :] = jnp.where(lane < HEAD_DIM, q, jnp.zeros_like(q))
    qq_ref[tq:2 * tq, :] = jnp.where(lane < HEAD_DIM, jnp.zeros_like(q), q)
    ones_rows = jnp.ones((DIFF_SUM_ROWS, t), BF16)
    even = (z0_ref, zmax0_ref, p0_ref, a0_ref)
    odd = (z1_ref, zmax1_ref, p1_ref, a1_ref)

    def key_start(n):
        return pl.multiple_of((n_tiles - 1 - n) * t, t)

    def scores(n, bufs, near=None):
        z_ref, zmax_ref, _, _ = bufs
        z = lax.dot_general(k_ref[pl.ds(key_start(n), t), :], qq_ref[...], NT_DIMS,
                            preferred_element_type=F32)
        if near is not None:
            z = z + bias_ref[near]
        z_ref[...] = z
        zmax_ref[...] = jnp.max(z, axis=0, keepdims=True)

    def softmax(bufs):
        z_ref, zmax_ref, p_ref, a_ref = bufs
        m_prev = m_ref[...]
        m_new = jnp.maximum(m_prev, zmax_ref[...])
        a_ref[...] = jnp.exp2(m_prev - m_new)
        p_ref[...] = jnp.exp2(z_ref[...] - m_new).astype(BF16)
        m_ref[...] = m_new

    def values(n, bufs):
        _, _, p_ref, a_ref = bufs
        vt = jnp.concatenate([vt_ref[:, pl.ds(key_start(n), t)], ones_rows], axis=0)
        acc_ref[...] = a_ref[...] * acc_ref[...] + jnp.dot(vt, p_ref[...],
                                                           preferred_element_type=F32)

    m_ref[...] = jnp.full_like(m_ref, MASKED_LOGIT)
    acc_ref[...] = jnp.zeros_like(acc_ref)
    scores(0, even, 0)
    scores(1, odd, 1)
    softmax(even)

    @pl.when(i > 0)
    def _():
        scores(2, even, 2)
        softmax(odd)
        values(0, even)
        scores(3, odd)
        softmax(even)
        values(1, odd)

        def pair(k, carry):
            n = 2 * k + 2
            scores(n, even)
            softmax(odd)
            values(n - 2, even)
            scores(n + 1, odd)
            softmax(even)
            values(n - 1, odd)
            return carry

        lax.fori_loop(1, i, pair, 0)

    softmax(odd)
    values(n_tiles - 2, even)
    values(n_tiles - 1, odd)

    lp = lam_ref[...]
    lam = (jnp.exp(jnp.sum(lp[0:1] * lp[1:2], axis=-1, keepdims=True))
           - jnp.exp(jnp.sum(lp[2:3] * lp[3:4], axis=-1, keepdims=True)) + lambda_init)
    normalised = acc_ref[0:LANES, :] / acc_ref[LANES:LANES + 1, :]
    o = normalised[:, :tq] - lam * normalised[:, tq:]
    o = o * lax.rsqrt(jnp.mean(o * o, axis=0, keepdims=True) + NORM_EPS)
    o_ref[...] = (o.T * g_ref[...] * (1.0 - lambda_init)).astype(o_ref.dtype)


def _diff_attention(qk, vt, bias_tiles, lam_params, g, lambda_init):
    b, s, _ = qk.shape
    t = ATT_TILE
    tq = DIFF_Q_TILE
    q0 = 2 * SB_WIDTH // LANES
    v0 = SB_WIDTH // LANES
    tile_f32 = pltpu.VMEM((t, 2 * tq), F32)
    tile_bf16 = pltpu.VMEM((t, 2 * tq), BF16)
    row_f32 = pltpu.VMEM((1, 2 * tq), F32)
    return pl.pallas_call(
        functools.partial(_diff_kernel, lambda_init=lambda_init),
        grid=(b, DIFF_HEADS, s // tq),
        in_specs=[
            pl.BlockSpec((None, tq, LANES), lambda bi, h, i: (bi, i, q0 + h)),
            pl.BlockSpec((None, s, LANES), lambda bi, h, i: (bi, 0, q0 + DIFF_HEADS + h)),
            pl.BlockSpec((None, LANES, s), lambda bi, h, i: (bi, v0 + h, 0)),
            pl.BlockSpec((None, DIFF_NEAR_TILES, t, 2 * tq), lambda bi, h, i: (h, 0, 0, 0)),
            pl.BlockSpec((4, HEAD_DIM), lambda bi, h, i: (0, 0)),
            pl.BlockSpec((1, LANES), lambda bi, h, i: (0, 0)),
        ],
        out_specs=pl.BlockSpec((None, tq, LANES), lambda bi, h, i: (bi, i, h)),
        out_shape=jax.ShapeDtypeStruct((b, s, DIFF_WIDTH), BF16),
        scratch_shapes=[pltpu.VMEM((2 * tq, LANES), BF16), tile_f32, tile_f32, row_f32, row_f32,
                        tile_bf16, tile_bf16, row_f32, row_f32, row_f32,
                        pltpu.VMEM((LANES + DIFF_SUM_ROWS, 2 * tq), F32)],
        compiler_params=_params("parallel", "parallel", "arbitrary"),
        name="diff_attention",
    )(qk, qk, vt, bias_tiles, lam_params, g)


def _t5_causal_bucket(n):
    max_exact = N_BUCKETS // 2
    nf = jnp.maximum(n, 1).astype(F32)
    large = max_exact + (jnp.log(nf / max_exact) / math.log(MAX_DISTANCE / max_exact)
                         * (N_BUCKETS - max_exact)).astype(jnp.int32)
    large = jnp.minimum(large, N_BUCKETS - 1)
    return jnp.where(n < max_exact, n, large)


def _bias_tiles(rel_bias):
    t = ATT_TILE
    assert MAX_DISTANCE <= t and DIFF_NEAR_TILES == DIFF_Q_TILE // t + 1
    key = jnp.arange(t, dtype=jnp.int32)[:, None]
    qry = jnp.arange(DIFF_Q_TILE, dtype=jnp.int32)[None, :]
    dist = jnp.stack([qry - key - (DIFF_Q_TILE - (n + 1) * t) for n in range(DIFF_NEAR_TILES)])
    bucket = _t5_causal_bucket(jnp.maximum(dist, 0))
    table = (rel_bias.astype(F32) - rel_bias.astype(F32)[N_BUCKETS - 1][None, :]) * LOG2_E
    near = jnp.zeros((DIFF_HEADS,) + dist.shape, F32)
    for k in range(N_BUCKETS - 1):
        near = jnp.where(bucket[None] == k, table[k][:, None, None, None], near)
    near = jnp.where((dist >= 0)[None], near, MASKED_LOGIT)
    return jnp.concatenate([near, near], axis=-1)


def _route_top2(h, rw_ref):
    h_hi = h.astype(BF16)
    h_lo = (h - h_hi.astype(F32)).astype(BF16)
    both = jnp.dot(h_hi, rw_ref[...], preferred_element_type=F32)
    logits = (both[:, :LANES] + both[:, LANES:]
              + jnp.dot(h_lo, rw_ref[:, :LANES], preferred_element_type=F32))
    lane = lax.broadcasted_iota(jnp.int32, logits.shape, 1)
    neg = -jnp.inf
    lg = jnp.where(lane < N_EXPERTS, logits, neg)
    m1 = jnp.max(lg, axis=-1, keepdims=True)
    i1 = jnp.min(jnp.where(lg == m1, lane, LANES), axis=-1, keepdims=True)
    lg2 = jnp.where(lane == i1, neg, lg)
    m2 = jnp.max(lg2, axis=-1, keepdims=True)
    i2 = jnp.min(jnp.where(lg2 == m2, lane, LANES), axis=-1, keepdims=True)
    e = jnp.exp(m2 - m1)
    g1 = 1.0 / (1.0 + e)
    g2 = e * g1
    return jnp.where(lane == 0, i1.astype(F32),
                     jnp.where(lane == 1, i2.astype(F32),
                               jnp.where(lane == 2, g1, jnp.where(lane == 3, g2, 0.0))))


def _out_proj_kernel(x_ref, sb_ref, df_ref, wo_ref, g_ref, *rest, route):
    if route:
        rw_ref, xo_ref, h_ref, route_ref = rest
    else:
        xo_ref, h_ref = rest
    y = (jnp.dot(sb_ref[...], wo_ref[:SB_WIDTH, :], preferred_element_type=F32)
         + jnp.dot(df_ref[...], wo_ref[SB_WIDTH:, :], preferred_element_type=F32))
    xn = x_ref[...] + y
    xo_ref[...] = xn
    h = _rms(xn, g_ref[...])
    h_ref[...] = h.astype(h_ref.dtype)
    if route:
        route_ref[...] = _route_top2(h, rw_ref)


def _out_proj(x, sb, df, wo, g, router_w=None):
    n = x.shape[0]
    route = router_w is not None
    rows = lambda w: pl.BlockSpec((ROW_TILE, w), lambda i: (i, 0))
    whole = lambda a: pl.BlockSpec(a.shape, lambda i: (0, 0))
    in_specs = [rows(D_MODEL), rows(SB_WIDTH), rows(DIFF_WIDTH), whole(wo), whole(g)]
    out_specs = [rows(D_MODEL), rows(D_MODEL)]
    out_shape = [jax.ShapeDtypeStruct((n, D_MODEL), F32),
                 jax.ShapeDtypeStruct((n, D_MODEL), F32 if route else BF16)]
    args = [x, sb, df, wo, g]
    if route:
        in_specs.append(whole(router_w))
        out_specs.append(rows(LANES))
        out_shape.append(jax.ShapeDtypeStruct((n, LANES), F32))
        args.append(router_w)
    return pl.pallas_call(
        functools.partial(_out_proj_kernel, route=route),
        grid=(n // ROW_TILE,),
        in_specs=in_specs,
        out_specs=out_specs,
        out_shape=out_shape,
        compiler_params=_params("parallel"),
        name="out_proj_route" if route else "out_proj",
    )(*args)


def _dense_ffn_kernel(x_ref, h_ref, wg_ref, wu_ref, wd_ref, o_ref):
    h = h_ref[...]
    gate = jnp.dot(h, wg_ref[...], preferred_element_type=F32)
    up = jnp.dot(h, wu_ref[...], preferred_element_type=F32)
    act = (gate * jax.nn.sigmoid(gate) * up).astype(BF16)
    o_ref[...] = x_ref[...] + jnp.dot(act, wd_ref[...], preferred_element_type=F32)


def _dense_ffn(x, h, wg, wu, wd):
    n = x.shape[0]
    tm = ROW_TILE // 2
    rows = pl.BlockSpec((tm, D_MODEL), lambda i: (i, 0))
    whole = lambda a: pl.BlockSpec(a.shape, lambda i: (0, 0), pipeline_mode=pl.Buffered(1))
    return pl.pallas_call(
        _dense_ffn_kernel,
        grid=(n // tm,),
        in_specs=[rows, rows, whole(wg), whole(wu), whole(wd)],
        out_specs=rows,
        out_shape=jax.ShapeDtypeStruct((n, D_MODEL), F32),
        compiler_params=_params("parallel"),
        name="dense_ffn",
    )(x, h, wg, wu, wd)


def _expert_kernel(be_ref, src_ref, dst_ref, h_ref, wg_ref, wu_ref, wd_ref, y_ref,
                   xbuf, xb_ref, obuf, acc_ref, gsem, ssem):
    tm = MOE_TILE
    i = pl.program_id(0)
    f = pl.program_id(1)
    nb = pl.num_programs(0)
    nf = pl.num_programs(1)
    chunk = tm // MOE_FF_STEPS
    slot = i % 2
    other = 1 - slot

    def gather_row(r, row):
        return pltpu.make_async_copy(h_ref.at[pl.ds(row, 1), :], xbuf.at[pl.ds(r, 1), :], gsem)

    def scatter_row(slot_id, r, row):
        return pltpu.make_async_copy(obuf.at[slot_id, pl.ds(r, 1), :], y_ref.at[pl.ds(row, 1), :],
                                     ssem.at[slot_id])

    def wait_gather():
        pltpu.make_async_copy(h_ref.at[pl.ds(0, tm), :], xbuf.at[pl.ds(0, tm), :], gsem).wait()

    def wait_scatter(slot_id):
        pltpu.make_async_copy(obuf.at[slot_id], y_ref.at[pl.ds(0, tm), :], ssem.at[slot_id]).wait()

    @pl.when((i == 0) & (f == 0))
    def _():
        obuf[...] = jnp.zeros_like(obuf)
        xbuf[tm:, :] = jnp.zeros((SPARE_ROWS, D_MODEL), F32)

        def first(r, carry):
            gather_row(r, src_ref[r]).start()
            return carry

        lax.fori_loop(0, tm, first, 0)

    @pl.when(f == 0)
    def _():
        wait_gather()
        xb_ref[...] = xbuf[0:tm, :].astype(BF16)
        acc_ref[...] = jnp.zeros_like(acc_ref)

    x = xb_ref[...]
    n_slices = MOE_FF_TILE // MXU_WIDTH
    done = 0
    anchor = None
    for c in range(n_slices):
        cols = slice(c * MXU_WIDTH, (c + 1) * MXU_WIDTH)
        gate = jnp.dot(x, wg_ref[:, cols], preferred_element_type=F32)
        up = jnp.dot(x, wu_ref[:, cols], preferred_element_type=F32)
        if anchor is not None:
            up = up + anchor
        act = (gate * jax.nn.sigmoid(gate) * up).astype(BF16)
        acc_ref[...] += jnp.dot(act, wd_ref[cols, :], preferred_element_type=F32)
        if c < n_slices - 1:
            upto = chunk * (c + 1) // (n_slices - 1)
            for r in range(done, upto):
                rr = f * chunk + r
                gather_row(rr, src_ref[(i + 1) * tm + rr]).start()
            done = upto
            anchor = jnp.where(i < 0, xbuf[tm:tm + 1, 0:MXU_WIDTH], 0.0)

    for r in range(chunk):
        rr = f * chunk + r
        scatter_row(other, rr, dst_ref[i * tm + rr]).start()

    @pl.when(f == nf - 1)
    def _():
        @pl.when(i >= 1)
        def _():
            wait_scatter(slot)

        obuf[slot] = acc_ref[...]

        @pl.when(i == nb - 1)
        def _():
            def last(r, carry):
                scatter_row(slot, r, dst_ref[(i + 1) * tm + r]).start()
                return carry

            lax.fori_loop(0, tm, last, 0)
            wait_scatter(slot)
            wait_scatter(other)
            wait_gather()


def _expert_ffn(h, block_expert, src_rows, dst_rows, wg, wu, wd):
    n_slots = src_rows.shape[0]
    nb = n_slots // MOE_TILE - 1
    d_ff = wg.shape[-1]
    nf = d_ff // MOE_FF_TILE
    assert nf == MOE_FF_STEPS and MOE_TILE % nf == 0
    return pl.pallas_call(
        _expert_kernel,
        grid_spec=pltpu.PrefetchScalarGridSpec(
            num_scalar_prefetch=3,
            grid=(nb, nf),
            in_specs=[
                pl.BlockSpec(memory_space=pl.ANY),
                pl.BlockSpec((None, D_MODEL, MOE_FF_TILE), lambda i, f, be, s, d: (be[i], 0, f)),
                pl.BlockSpec((None, D_MODEL, MOE_FF_TILE), lambda i, f, be, s, d: (be[i], 0, f)),
                pl.BlockSpec((None, MOE_FF_TILE, D_MODEL), lambda i, f, be, s, d: (be[i], f, 0)),
            ],
            out_specs=pl.BlockSpec(memory_space=pl.ANY),
            scratch_shapes=[pltpu.VMEM((MOE_TILE + SPARE_ROWS, D_MODEL), F32),
                            pltpu.VMEM((MOE_TILE, D_MODEL), BF16),
                            pltpu.VMEM((2, MOE_TILE, D_MODEL), F32),
                            pltpu.VMEM((MOE_TILE, D_MODEL), F32),
                            pltpu.SemaphoreType.DMA(()), pltpu.SemaphoreType.DMA((2,))],
        ),
        out_shape=jax.ShapeDtypeStruct((n_slots, D_MODEL), F32),
        compiler_params=_params("arbitrary", "arbitrary"),
        name="expert_ffn",
    )(block_expert, src_rows, dst_rows, h, wg, wu, wd)


def _combine_kernel(x_ref, y0_ref, y1_ref, route_ref, g_ref, o_ref, *, final):
    r = route_ref[...]
    out = x_ref[...] + r[:, 2:3] * y0_ref[...] + r[:, 3:4] * y1_ref[...]
    if final:
        out = _rms(out, g_ref[...])
    o_ref[...] = out


def _combine(x, y, route, g, final):
    n = x.shape[0]
    second = n // ROW_TILE
    rows = lambda w: pl.BlockSpec((ROW_TILE, w), lambda i: (i, 0))
    return pl.pallas_call(
        functools.partial(_combine_kernel, final=final),
        grid=(n // ROW_TILE,),
        in_specs=[rows(D_MODEL), rows(D_MODEL),
                  pl.BlockSpec((ROW_TILE, D_MODEL), lambda i: (second + i, 0)),
                  rows(LANES), pl.BlockSpec((1, D_MODEL), lambda i: (0, 0))],
        out_specs=rows(D_MODEL),
        out_shape=jax.ShapeDtypeStruct((n, D_MODEL), F32),
        compiler_params=_params("parallel"),
        name="moe_combine",
    )(x, y, y, route, g)


def _moe(x, h, route, wg, wu, wd, final_g, final):
    n = x.shape[0]
    n_assign = n * TOP_K
    n_blocks = n_assign // MOE_TILE + N_EXPERTS
    cap = n_blocks * MOE_TILE
    experts = route[:, :TOP_K].astype(jnp.int32).T.reshape(n_assign)
    onehot = (experts[:, None] == jnp.arange(N_EXPERTS, dtype=jnp.int32)[None, :]).astype(jnp.int32)
    running = jnp.cumsum(onehot, axis=0)
    rank = jnp.sum(running * onehot, axis=1) - 1
    counts = running[-1]
    padded = (counts + MOE_TILE - 1) // MOE_TILE * MOE_TILE
    pad_ends = jnp.cumsum(padded)
    pad_starts = pad_ends - padded
    slot_of = jnp.sum(onehot * pad_starts[None, :], axis=1) + rank
    assign_of = jnp.full((cap,), -1, jnp.int32).at[slot_of].set(
        jnp.arange(n_assign, dtype=jnp.int32))
    empty = assign_of < 0
    spare = n_assign + jnp.cumsum(empty.astype(jnp.int32)) - 1
    dst = jnp.where(empty, spare, assign_of)
    src = jnp.where(empty, 0, assign_of % n)
    lead = cap + jnp.arange(MOE_TILE, dtype=jnp.int32)
    dst_rows = jnp.concatenate([lead, dst]).astype(jnp.int32)
    src_rows = jnp.concatenate([src, jnp.zeros((MOE_TILE,), jnp.int32)]).astype(jnp.int32)
    block_start = jnp.arange(n_blocks, dtype=jnp.int32) * MOE_TILE
    block_expert = jnp.minimum(
        jnp.sum((block_start[:, None] >= pad_ends[None, :]).astype(jnp.int32), axis=1),
        N_EXPERTS - 1).astype(jnp.int32)

    y = _expert_ffn(h, block_expert, src_rows, dst_rows, wg, wu, wd)
    return _combine(x, y, route, final_g, final)


def _final_norm_kernel(x_ref, g_ref, o_ref):
    o_ref[...] = _rms(x_ref[...], g_ref[...])


def _final_norm(x, g):
    n = x.shape[0]
    rows = pl.BlockSpec((ROW_TILE, D_MODEL), lambda i: (i, 0))
    return pl.pallas_call(
        _final_norm_kernel,
        grid=(n // ROW_TILE,),
        in_specs=[rows, pl.BlockSpec((1, D_MODEL), lambda i: (0, 0))],
        out_specs=rows,
        out_shape=jax.ShapeDtypeStruct((n, D_MODEL), F32),
        compiler_params=_params("parallel"),
        name="final_norm",
    )(x, g)


def kernel(x, w_in, w_out, attn_norm, ffn_norm, sb_out_norm, diff_subln, lambda_q1, lambda_k1,
           lambda_q2, lambda_k2, rel_bias, dense_w_gate, dense_w_up, dense_w_down, router_w,
           expert_w_gate, expert_w_up, expert_w_down, final_norm):
    b, s, d = x.shape
    depth = w_in.shape[0]
    n = b * s
    assert d == D_MODEL and s % DIFF_Q_TILE == 0 and s % ROW_TILE == 0
    assert (n * TOP_K) % MOE_TILE == 0

    scale = HEAD_DIM ** -0.5
    sb_q, sb_k, sb_v = 0, SB_WIDTH, 2 * SB_WIDTH
    df_q, df_k, df_v = 3 * SB_WIDTH, 3 * SB_WIDTH + DIFF_WIDTH, 3 * SB_WIDTH + 2 * DIFF_WIDTH
    bias_tiles = _bias_tiles(rel_bias)
    row2 = lambda v: v.astype(F32).reshape(1, -1)

    xf = x.reshape(n, d)
    for i in range(depth):
        last = i == depth - 1
        w = w_in[i]
        wqk = jnp.concatenate([w[:, sb_q:sb_k] * (scale * LOG2_E), w[:, sb_k:sb_v],
                               w[:, df_q:df_k] * (scale * LOG2_E), w[:, df_k:df_v]],
                              axis=1).astype(BF16)
        wvt = jnp.concatenate([w[:, sb_v:df_q], w[:, df_v:]], axis=1).T.astype(BF16)
        qk, vt = _norm_proj(xf, row2(attn_norm[i]), wqk, wvt, b, s)
        qk = qk.reshape(b, s, QK_WIDTH)
        sb = _sb_attention(qk, vt, row2(jnp.tile(sb_out_norm[i], 2)))
        lambda_init = 0.8 - 0.6 * math.exp(-0.3 * i)
        lam_params = jnp.stack([lambda_q1[i], lambda_k1[i], lambda_q2[i], lambda_k2[i]]).astype(F32)
        df = _diff_attention(qk, vt, bias_tiles, lam_params, row2(diff_subln[i]), lambda_init)
        sb = sb.reshape(n, SB_WIDTH)
        df = df.reshape(n, DIFF_WIDTH)
        wo = w_out[i].astype(BF16)
        j = i // 2
        if i % 2 == 0:
            xf, h = _out_proj(xf, sb, df, wo, row2(ffn_norm[i]))
            xf = _dense_ffn(xf, h, dense_w_gate[j].astype(BF16), dense_w_up[j].astype(BF16),
                            dense_w_down[j].astype(BF16))
            if last:
                xf = _final_norm(xf, row2(final_norm))
        else:
            rw = jnp.pad(router_w[j].astype(F32), ((0, 0), (0, LANES - N_EXPERTS)))
            rw_hi = rw.astype(BF16)
            rw_lo = (rw - rw_hi.astype(F32)).astype(BF16)
            xf, h, route = _out_proj(xf, sb, df, wo, row2(ffn_norm[i]),
                                     jnp.concatenate([rw_hi, rw_lo], axis=1))
            xf = _moe(xf, h, route, expert_w_gate[j].astype(BF16), expert_w_up[j].astype(BF16),
                      expert_w_down[j].astype(BF16), row2(final_norm), last)
    return xf.reshape(b, s, d)
```

```python
import functools
import math

import jax
import jax.numpy as jnp
from jax import lax
from jax.experimental import pallas as pl
from jax.experimental.pallas import tpu as pltpu

D_MODEL = 1024
HEAD_DIM = 64
SB_HEADS = 8
DIFF_HEADS = 4
SB_WIDTH = SB_HEADS * HEAD_DIM
DIFF_WIDTH = DIFF_HEADS * 2 * HEAD_DIM
QK_WIDTH = 2 * SB_WIDTH + 2 * DIFF_WIDTH
V_WIDTH = SB_WIDTH + DIFF_WIDTH
N_BUCKETS = 32
MAX_DISTANCE = 128
N_EXPERTS = 8
TOP_K = 2
NORM_EPS = 1e-6

LANES = 128
MXU_WIDTH = 256
SPARE_ROWS = 8
VMEM_LIMIT = 56 * 1024 * 1024

ROW_TILE = 512
ATT_TILE = 256
DIFF_Q_TILE = 512
DIFF_NEAR_TILES = 3
DIFF_GROUP = 4
MOE_TILE = 512
MOE_FF_TILE = 1792
MOE_FF_STEPS = 2
SB_LOG_WEIGHT_FLOOR = -150.0
MASKED_LOGIT = -1e30
LOG2_E = math.log2(math.e)
DIFF_SUM_ROWS = 16

F32 = jnp.float32
BF16 = jnp.bfloat16
NT_DIMS = (((1,), (1,)), ((), ()))


def _params(*semantics):
    return pltpu.CompilerParams(dimension_semantics=semantics, vmem_limit_bytes=VMEM_LIMIT)


def _rms(x, g):
    return x * lax.rsqrt(jnp.mean(x * x, axis=-1, keepdims=True) + NORM_EPS) * g


def _norm_proj_kernel(x_ref, g_ref, wqk_ref, wvt_ref, qk_ref, vt_ref):
    h = _rms(x_ref[...], g_ref[...]).astype(BF16)
    for c in range(QK_WIDTH // D_MODEL):
        cols = slice(c * D_MODEL, (c + 1) * D_MODEL)
        qk_ref[:, cols] = jnp.dot(h, wqk_ref[:, cols], preferred_element_type=F32).astype(BF16)
    vt_ref[...] = lax.dot_general(wvt_ref[...], h, NT_DIMS,
                                  preferred_element_type=F32).astype(BF16)


def _norm_proj(x, g, wqk, wvt, batch, seq):
    n = x.shape[0]
    per_seq = seq // ROW_TILE
    whole = lambda a: pl.BlockSpec(a.shape, lambda b, i: (0, 0), pipeline_mode=pl.Buffered(1))
    return pl.pallas_call(
        _norm_proj_kernel,
        grid=(batch, per_seq),
        in_specs=[
            pl.BlockSpec((ROW_TILE, D_MODEL), lambda b, i: (b * per_seq + i, 0)),
            pl.BlockSpec((1, D_MODEL), lambda b, i: (0, 0)),
            whole(wqk), whole(wvt),
        ],
        out_specs=[
            pl.BlockSpec((ROW_TILE, QK_WIDTH), lambda b, i: (b * per_seq + i, 0)),
            pl.BlockSpec((None, V_WIDTH, ROW_TILE), lambda b, i: (b, 0, i)),
        ],
        out_shape=[jax.ShapeDtypeStruct((n, QK_WIDTH), BF16),
                   jax.ShapeDtypeStruct((batch, V_WIDTH, seq), BF16)],
        compiler_params=_params("parallel", "parallel"),
        name="norm_proj",
    )(x, g, wqk, wvt)


def _sb_kernel(q_ref, k_ref, vt_ref, g_ref, o_ref, qq_ref, acc_ref):
    t = ATT_TILE
    i = pl.program_id(2)
    q = q_ref[...]
    lane = lax.broadcasted_iota(jnp.int32, (1, LANES), 1)
    qq_ref[0:t, :] = jnp.where(lane < HEAD_DIM, q, jnp.zeros_like(q))
    qq_ref[t:2 * t, :] = jnp.where(lane < HEAD_DIM, jnp.zeros_like(q), q)
    key = lax.broadcasted_iota(jnp.int32, (t, t), 0)
    pos = lax.broadcasted_iota(jnp.int32, (t, t), 1)
    suffix_ones = (pos >= key).astype(BF16)
    key2 = lax.broadcasted_iota(jnp.int32, (t, 2 * t), 0)
    qry2 = lax.broadcasted_iota(jnp.int32, (t, 2 * t), 1)
    strictly_causal = key2 < jnp.where(qry2 >= t, qry2 - t, qry2)

    def scores(j):
        start = pl.multiple_of(j * t, t)
        return lax.dot_general(k_ref[pl.ds(start, t), :], qq_ref[...], NT_DIMS,
                               preferred_element_type=F32)

    def split_log_stay(z, diagonal):
        neg_z = -z
        log_stay = jnp.minimum(neg_z, 0.0) - jnp.log2(1.0 + jnp.exp2(jnp.minimum(z, neg_z)))
        if diagonal:
            log_stay = jnp.where(strictly_causal, log_stay, 0.0)
        hi = log_stay.astype(BF16)
        return hi, (log_stay - hi.astype(F32)).astype(BF16)

    def suffix_sums(hi, lo):
        return (jnp.dot(suffix_ones, hi, preferred_element_type=F32)
                + jnp.dot(suffix_ones, lo, preferred_element_type=F32))

    def weights(z, incl, carry, diagonal):
        log_a = z + incl
        if carry is not None:
            log_a = log_a + carry
        a = jnp.exp2(log_a)
        if diagonal:
            a = jnp.where(strictly_causal, a, 0.0)
        return a.astype(BF16), incl[0:1, :] if carry is None else carry + incl[0:1, :]

    def accumulate(j, a, first):
        start = pl.multiple_of(j * t, t)
        update = jnp.dot(vt_ref[:, pl.ds(start, t)], a, preferred_element_type=F32)
        if first:
            acc_ref[...] = update
        else:
            acc_ref[...] += update

    def tile(j, carry, diagonal):
        z = scores(j)
        a, carry = weights(z, suffix_sums(*split_log_stay(z, diagonal)), carry, diagonal)
        accumulate(j, a, diagonal)
        return carry

    @pl.when(i == 0)
    def _():
        tile(i, None, True)

    @pl.when(i > 0)
    def _():
        z_d = scores(i)
        z_p = scores(i - 1)
        split_d = split_log_stay(z_d, True)
        incl_d = suffix_sums(*split_d)
        split_p = split_log_stay(z_p, False)
        a_d, carry = weights(z_d, incl_d, None, True)
        incl_p = suffix_sums(*split_p)
        accumulate(i, a_d, True)
        a_p, carry = weights(z_p, incl_p, carry, False)
        accumulate(i - 1, a_p, False)

        def cond(state):
            j, _, top_carry = state
            return (j >= 0) & (top_carry > SB_LOG_WEIGHT_FLOOR)

        def body(state):
            j, carry, _ = state
            carry = tile(j, carry, False)
            return j - 1, carry, jnp.max(carry)

        lax.while_loop(cond, body, (i - 2, carry, jnp.max(carry)))

    normed = []
    for h in range(2):
        o = acc_ref[h * HEAD_DIM:(h + 1) * HEAD_DIM, h * t:(h + 1) * t]
        ms = jnp.mean(o * o, axis=0, keepdims=True)
        normed.append(o * lax.rsqrt(ms + NORM_EPS))
    o_ref[...] = (jnp.concatenate(normed, axis=0).T * g_ref[...]).astype(o_ref.dtype)


def _sb_attention(qk, vt, g2):
    b, s, _ = qk.shape
    t = ATT_TILE
    pairs = SB_WIDTH // LANES
    return pl.pallas_call(
        _sb_kernel,
        grid=(b, pairs, s // t),
        in_specs=[
            pl.BlockSpec((None, t, LANES), lambda bi, p, i: (bi, i, p)),
            pl.BlockSpec((None, s, LANES), lambda bi, p, i: (bi, 0, pairs + p)),
            pl.BlockSpec((None, LANES, s), lambda bi, p, i: (bi, p, 0)),
            pl.BlockSpec((1, LANES), lambda bi, p, i: (0, 0)),
        ],
        out_specs=pl.BlockSpec((None, t, LANES), lambda bi, p, i: (bi, i, p)),
        out_shape=jax.ShapeDtypeStruct((b, s, SB_WIDTH), BF16),
        scratch_shapes=[pltpu.VMEM((2 * t, LANES), BF16), pltpu.VMEM((LANES, 2 * t), F32)],
        compiler_params=_params("parallel", "parallel", "arbitrary"),
        name="sb_attention",
    )(qk, qk, vt, g2)


def _diff_kernel(q_ref, k_ref, vt_ref, bias_ref, lam_ref, g_ref, o_ref, *scratch, lambda_init):
    t = ATT_TILE
    tq = DIFF_Q_TILE
    i = pl.program_id(2)
    n_tiles = (tq // t) * (i + 1)
    heads = range(DIFF_GROUP)
    per_head = len(scratch) // DIFF_GROUP
    bufs = [scratch[h * per_head:(h + 1) * per_head] for h in heads]
    qq = [b[0] for b in bufs]
    even = [b[1:5] for b in bufs]
    odd = [b[5:9] for b in bufs]
    m_refs = [b[9] for b in bufs]
    acc_refs = [b[10] for b in bufs]
    lane = lax.broadcasted_iota(jnp.int32, (1, LANES), 1)
    ones_rows = jnp.ones((DIFF_SUM_ROWS, t), BF16)
    for h in heads:
        q = q_ref[:, h * LANES:(h + 1) * LANES]
        qq[h][0:tq, :] = jnp.where(lane < HEAD_DIM, q, jnp.zeros_like(q))
        qq[h][tq:2 * tq, :] = jnp.where(lane < HEAD_DIM, jnp.zeros_like(q), q)
        m_refs[h][...] = jnp.full_like(m_refs[h], MASKED_LOGIT)
        acc_refs[h][...] = jnp.zeros_like(acc_refs[h])

    def key_start(n):
        return pl.multiple_of((n_tiles - 1 - n) * t, t)

    def scores(n, parity, near=None):
        for h in heads:
            z_ref, zmax_ref, _, _ = parity[h]
            z = lax.dot_general(k_ref[pl.ds(key_start(n), t), h * LANES:(h + 1) * LANES],
                                qq[h][...], NT_DIMS, preferred_element_type=F32)
            if near is not None:
                z = z + bias_ref[h, near]
            z_ref[...] = z
            zmax_ref[...] = jnp.max(z, axis=0, keepdims=True)

    def softmax(parity):
        for h in heads:
            z_ref, zmax_ref, p_ref, a_ref = parity[h]
            m_prev = m_refs[h][...]
            m_new = jnp.maximum(m_prev, zmax_ref[...])
            a_ref[...] = jnp.exp2(m_prev - m_new)
            p_ref[...] = jnp.exp2(z_ref[...] - m_new).astype(BF16)
            m_refs[h][...] = m_new

    def values(n, parity):
        for h in heads:
            _, _, p_ref, a_ref = parity[h]
            vt = jnp.concatenate([vt_ref[h * LANES:(h + 1) * LANES, pl.ds(key_start(n), t)],
                                  ones_rows], axis=0)
            acc_refs[h][...] = a_ref[...] * acc_refs[h][...] + jnp.dot(
                vt, p_ref[...], preferred_element_type=F32)

    scores(0, even, 0)
    scores(1, odd, 1)
    softmax(even)

    @pl.when(i > 0)
    def _():
        scores(2, even, 2)
        softmax(odd)
        values(0, even)
        scores(3, odd)
        softmax(even)
        values(1, odd)

        def pair(k, carry):
            n = 2 * k + 2
            scores(n, even)
            softmax(odd)
            values(n - 2, even)
            scores(n + 1, odd)
            softmax(even)
            values(n - 1, odd)
            return carry

        lax.fori_loop(1, i, pair, 0)

    softmax(odd)
    values(n_tiles - 2, even)
    values(n_tiles - 1, odd)

    lp = lam_ref[...]
    lam = (jnp.exp(jnp.sum(lp[0:1] * lp[1:2], axis=-1, keepdims=True))
           - jnp.exp(jnp.sum(lp[2:3] * lp[3:4], axis=-1, keepdims=True)) + lambda_init)
    for h in heads:
        acc_ref = acc_refs[h]
        normalised = acc_ref[0:LANES, :] / acc_ref[LANES:LANES + 1, :]
        o = normalised[:, :tq] - lam * normalised[:, tq:]
        o = o * lax.rsqrt(jnp.mean(o * o, axis=0, keepdims=True) + NORM_EPS)
        o_ref[:, h * LANES:(h + 1) * LANES] = (
            o.T * g_ref[...] * (1.0 - lambda_init)).astype(o_ref.dtype)


def _diff_attention(qk, vt, bias_tiles, lam_params, g, lambda_init):
    b, s, _ = qk.shape
    t = ATT_TILE
    tq = DIFF_Q_TILE
    width = DIFF_GROUP * LANES
    q0 = 2 * SB_WIDTH // width
    v0 = SB_WIDTH // width
    groups = DIFF_HEADS // DIFF_GROUP
    tile_f32 = pltpu.VMEM((t, 2 * tq), F32)
    tile_bf16 = pltpu.VMEM((t, 2 * tq), BF16)
    row_f32 = pltpu.VMEM((1, 2 * tq), F32)
    stage = [tile_f32, row_f32, tile_bf16, row_f32]
    per_head = ([pltpu.VMEM((2 * tq, LANES), BF16)] + stage + stage
                + [row_f32, pltpu.VMEM((LANES + DIFF_SUM_ROWS, 2 * tq), F32)])
    return pl.pallas_call(
        functools.partial(_diff_kernel, lambda_init=lambda_init),
        grid=(b, groups, s // tq),
        in_specs=[
            pl.BlockSpec((None, tq, width), lambda bi, hg, i: (bi, i, q0 + hg)),
            pl.BlockSpec((None, s, width), lambda bi, hg, i: (bi, 0, q0 + groups + hg),
                         pipeline_mode=pl.Buffered(1)),
            pl.BlockSpec((None, width, s), lambda bi, hg, i: (bi, v0 + hg, 0),
                         pipeline_mode=pl.Buffered(1)),
            pl.BlockSpec((DIFF_GROUP, DIFF_NEAR_TILES, t, 2 * tq), lambda bi, hg, i: (hg, 0, 0, 0),
                         pipeline_mode=pl.Buffered(1)),
            pl.BlockSpec((4, HEAD_DIM), lambda bi, hg, i: (0, 0)),
            pl.BlockSpec((1, LANES), lambda bi, hg, i: (0, 0)),
        ],
        out_specs=pl.BlockSpec((None, tq, width), lambda bi, hg, i: (bi, i, hg)),
        out_shape=jax.ShapeDtypeStruct((b, s, DIFF_WIDTH), BF16),
        scratch_shapes=per_head * DIFF_GROUP,
        compiler_params=_params("parallel", "parallel", "arbitrary"),
        name="diff_attention",
    )(qk, qk, vt, bias_tiles, lam_params, g)


def _t5_causal_bucket(n):
    max_exact = N_BUCKETS // 2
    nf = jnp.maximum(n, 1).astype(F32)
    large = max_exact + (jnp.log(nf / max_exact) / math.log(MAX_DISTANCE / max_exact)
                         * (N_BUCKETS - max_exact)).astype(jnp.int32)
    large = jnp.minimum(large, N_BUCKETS - 1)
    return jnp.where(n < max_exact, n, large)


def _bias_tiles(rel_bias):
    t = ATT_TILE
    assert MAX_DISTANCE <= t and DIFF_NEAR_TILES == DIFF_Q_TILE // t + 1
    key = jnp.arange(t, dtype=jnp.int32)[:, None]
    qry = jnp.arange(DIFF_Q_TILE, dtype=jnp.int32)[None, :]
    dist = jnp.stack([qry - key - (DIFF_Q_TILE - (n + 1) * t) for n in range(DIFF_NEAR_TILES)])
    bucket = _t5_causal_bucket(jnp.maximum(dist, 0))
    table = (rel_bias.astype(F32) - rel_bias.astype(F32)[N_BUCKETS - 1][None, :]) * LOG2_E
    near = jnp.zeros((DIFF_HEADS,) + dist.shape, F32)
    for k in range(N_BUCKETS - 1):
        near = jnp.where(bucket[None] == k, table[k][:, None, None, None], near)
    near = jnp.where((dist >= 0)[None], near, MASKED_LOGIT)
    return jnp.concatenate([near, near], axis=-1)


def _route_top2(h, rw_ref):
    h_hi = h.astype(BF16)
    h_lo = (h - h_hi.astype(F32)).astype(BF16)
    both = jnp.dot(h_hi, rw_ref[...], preferred_element_type=F32)
    logits = (both[:, :LANES] + both[:, LANES:]
              + jnp.dot(h_lo, rw_ref[:, :LANES], preferred_element_type=F32))
    lane = lax.broadcasted_iota(jnp.int32, logits.shape, 1)
    neg = -jnp.inf
    lg = jnp.where(lane < N_EXPERTS, logits, neg)
    m1 = jnp.max(lg, axis=-1, keepdims=True)
    i1 = jnp.min(jnp.where(lg == m1, lane, LANES), axis=-1, keepdims=True)
    lg2 = jnp.where(lane == i1, neg, lg)
    m2 = jnp.max(lg2, axis=-1, keepdims=True)
    i2 = jnp.min(jnp.where(lg2 == m2, lane, LANES), axis=-1, keepdims=True)
    e = jnp.exp(m2 - m1)
    g1 = 1.0 / (1.0 + e)
    g2 = e * g1
    return jnp.where(lane == 0, i1.astype(F32),
                     jnp.where(lane == 1, i2.astype(F32),
                               jnp.where(lane == 2, g1, jnp.where(lane == 3, g2, 0.0))))


def _out_proj_kernel(x_ref, sb_ref, df_ref, wo_ref, g_ref, *rest, route):
    if route:
        rw_ref, xo_ref, h_ref, route_ref = rest
    else:
        xo_ref, h_ref = rest
    y = (jnp.dot(sb_ref[...], wo_ref[:SB_WIDTH, :], preferred_element_type=F32)
         + jnp.dot(df_ref[...], wo_ref[SB_WIDTH:, :], preferred_element_type=F32))
    xn = x_ref[...] + y
    xo_ref[...] = xn
    h = _rms(xn, g_ref[...])
    h_ref[...] = h.astype(h_ref.dtype)
    if route:
        route_ref[...] = _route_top2(h, rw_ref)


def _out_proj(x, sb, df, wo, g, router_w=None):
    n = x.shape[0]
    route = router_w is not None
    rows = lambda w: pl.BlockSpec((ROW_TILE, w), lambda i: (i, 0))
    whole = lambda a: pl.BlockSpec(a.shape, lambda i: (0, 0))
    in_specs = [rows(D_MODEL), rows(SB_WIDTH), rows(DIFF_WIDTH), whole(wo), whole(g)]
    out_specs = [rows(D_MODEL), rows(D_MODEL)]
    out_shape = [jax.ShapeDtypeStruct((n, D_MODEL), F32),
                 jax.ShapeDtypeStruct((n, D_MODEL), F32 if route else BF16)]
    args = [x, sb, df, wo, g]
    if route:
        in_specs.append(whole(router_w))
        out_specs.append(rows(LANES))
        out_shape.append(jax.ShapeDtypeStruct((n, LANES), F32))
        args.append(router_w)
    return pl.pallas_call(
        functools.partial(_out_proj_kernel, route=route),
        grid=(n // ROW_TILE,),
        in_specs=in_specs,
        out_specs=out_specs,
        out_shape=out_shape,
        compiler_params=_params("parallel"),
        name="out_proj_route" if route else "out_proj",
    )(*args)


def _dense_ffn_kernel(x_ref, h_ref, wg_ref, wu_ref, wd_ref, o_ref):
    h = h_ref[...]
    gate = jnp.dot(h, wg_ref[...], preferred_element_type=F32)
    up = jnp.dot(h, wu_ref[...], preferred_element_type=F32)
    act = (gate * jax.nn.sigmoid(gate) * up).astype(BF16)
    o_ref[...] = x_ref[...] + jnp.dot(act, wd_ref[...], preferred_element_type=F32)


def _dense_ffn(x, h, wg, wu, wd):
    n = x.shape[0]
    tm = ROW_TILE // 2
    rows = pl.BlockSpec((tm, D_MODEL), lambda i: (i, 0))
    whole = lambda a: pl.BlockSpec(a.shape, lambda i: (0, 0), pipeline_mode=pl.Buffered(1))
    return pl.pallas_call(
        _dense_ffn_kernel,
        grid=(n // tm,),
        in_specs=[rows, rows, whole(wg), whole(wu), whole(wd)],
        out_specs=rows,
        out_shape=jax.ShapeDtypeStruct((n, D_MODEL), F32),
        compiler_params=_params("parallel"),
        name="dense_ffn",
    )(x, h, wg, wu, wd)


def _expert_kernel(be_ref, src_ref, dst_ref, h_ref, wg_ref, wu_ref, wd_ref, y_ref,
                   xbuf, xb_ref, obuf, acc_ref, gsem, ssem):
    tm = MOE_TILE
    i = pl.program_id(0)
    f = pl.program_id(1)
    nb = pl.num_programs(0)
    nf = pl.num_programs(1)
    chunk = tm // MOE_FF_STEPS
    slot = i % 2
    other = 1 - slot

    def gather_row(r, row):
        return pltpu.make_async_copy(h_ref.at[pl.ds(row, 1), :], xbuf.at[pl.ds(r, 1), :], gsem)

    def scatter_row(slot_id, r, row):
        return pltpu.make_async_copy(obuf.at[slot_id, pl.ds(r, 1), :], y_ref.at[pl.ds(row, 1), :],
                                     ssem.at[slot_id])

    def wait_gather():
        pltpu.make_async_copy(h_ref.at[pl.ds(0, tm), :], xbuf.at[pl.ds(0, tm), :], gsem).wait()

    def wait_scatter(slot_id):
        pltpu.make_async_copy(obuf.at[slot_id], y_ref.at[pl.ds(0, tm), :], ssem.at[slot_id]).wait()

    @pl.when((i == 0) & (f == 0))
    def _():
        obuf[...] = jnp.zeros_like(obuf)
        xbuf[tm:, :] = jnp.zeros((SPARE_ROWS, D_MODEL), F32)

        def first(r, carry):
            gather_row(r, src_ref[r]).start()
            return carry

        lax.fori_loop(0, tm, first, 0)

    @pl.when(f == 0)
    def _():
        wait_gather()
        xb_ref[...] = xbuf[0:tm, :].astype(BF16)
        acc_ref[...] = jnp.zeros_like(acc_ref)

    x = xb_ref[...]
    n_slices = MOE_FF_TILE // MXU_WIDTH
    done = 0
    anchor = None
    for c in range(n_slices):
        cols = slice(c * MXU_WIDTH, (c + 1) * MXU_WIDTH)
        gate = jnp.dot(x, wg_ref[:, cols], preferred_element_type=F32)
        up = jnp.dot(x, wu_ref[:, cols], preferred_element_type=F32)
        if anchor is not None:
            up = up + anchor
        act = (gate * jax.nn.sigmoid(gate) * up).astype(BF16)
        acc_ref[...] += jnp.dot(act, wd_ref[cols, :], preferred_element_type=F32)
        if c < n_slices - 1:
            upto = chunk * (c + 1) // (n_slices - 1)
            for r in range(done, upto):
                rr = f * chunk + r
                gather_row(rr, src_ref[(i + 1) * tm + rr]).start()
            done = upto
            anchor = jnp.where(i < 0, xbuf[tm:tm + 1, 0:MXU_WIDTH], 0.0)

    for r in range(chunk):
        rr = f * chunk + r
        scatter_row(other, rr, dst_ref[i * tm + rr]).start()

    @pl.when(f == nf - 1)
    def _():
        @pl.when(i >= 1)
        def _():
            wait_scatter(slot)

        obuf[slot] = acc_ref[...]

        @pl.when(i == nb - 1)
        def _():
            def last(r, carry):
                scatter_row(slot, r, dst_ref[(i + 1) * tm + r]).start()
                return carry

            lax.fori_loop(0, tm, last, 0)
            wait_scatter(slot)
            wait_scatter(other)
            wait_gather()


def _expert_ffn(h, block_expert, src_rows, dst_rows, wg, wu, wd):
    n_slots = src_rows.shape[0]
    nb = n_slots // MOE_TILE - 1
    d_ff = wg.shape[-1]
    nf = d_ff // MOE_FF_TILE
    assert nf == MOE_FF_STEPS and MOE_TILE % nf == 0
    return pl.pallas_call(
        _expert_kernel,
        grid_spec=pltpu.PrefetchScalarGridSpec(
            num_scalar_prefetch=3,
            grid=(nb, nf),
            in_specs=[
                pl.BlockSpec(memory_space=pl.ANY),
                pl.BlockSpec((None, D_MODEL, MOE_FF_TILE), lambda i, f, be, s, d: (be[i], 0, f)),
                pl.BlockSpec((None, D_MODEL, MOE_FF_TILE), lambda i, f, be, s, d: (be[i], 0, f)),
                pl.BlockSpec((None, MOE_FF_TILE, D_MODEL), lambda i, f, be, s, d: (be[i], f, 0)),
            ],
            out_specs=pl.BlockSpec(memory_space=pl.ANY),
            scratch_shapes=[pltpu.VMEM((MOE_TILE + SPARE_ROWS, D_MODEL), F32),
                            pltpu.VMEM((MOE_TILE, D_MODEL), BF16),
                            pltpu.VMEM((2, MOE_TILE, D_MODEL), F32),
                            pltpu.VMEM((MOE_TILE, D_MODEL), F32),
                            pltpu.SemaphoreType.DMA(()), pltpu.SemaphoreType.DMA((2,))],
        ),
        out_shape=jax.ShapeDtypeStruct((n_slots, D_MODEL), F32),
        compiler_params=_params("arbitrary", "arbitrary"),
        name="expert_ffn",
    )(block_expert, src_rows, dst_rows, h, wg, wu, wd)


def _combine_kernel(x_ref, y0_ref, y1_ref, route_ref, g_ref, o_ref, *, final):
    r = route_ref[...]
    out = x_ref[...] + r[:, 2:3] * y0_ref[...] + r[:, 3:4] * y1_ref[...]
    if final:
        out = _rms(out, g_ref[...])
    o_ref[...] = out


def _combine(x, y, route, g, final):
    n = x.shape[0]
    second = n // ROW_TILE
    rows = lambda w: pl.BlockSpec((ROW_TILE, w), lambda i: (i, 0))
    return pl.pallas_call(
        functools.partial(_combine_kernel, final=final),
        grid=(n // ROW_TILE,),
        in_specs=[rows(D_MODEL), rows(D_MODEL),
                  pl.BlockSpec((ROW_TILE, D_MODEL), lambda i: (second + i, 0)),
                  rows(LANES), pl.BlockSpec((1, D_MODEL), lambda i: (0, 0))],
        out_specs=rows(D_MODEL),
        out_shape=jax.ShapeDtypeStruct((n, D_MODEL), F32),
        compiler_params=_params("parallel"),
        name="moe_combine",
    )(x, y, y, route, g)


def _moe(x, h, route, wg, wu, wd, final_g, final):
    n = x.shape[0]
    n_assign = n * TOP_K
    n_blocks = n_assign // MOE_TILE + N_EXPERTS
    cap = n_blocks * MOE_TILE
    experts = route[:, :TOP_K].astype(jnp.int32).T.reshape(n_assign)
    onehot = (experts[:, None] == jnp.arange(N_EXPERTS, dtype=jnp.int32)[None, :]).astype(jnp.int32)
    running = jnp.cumsum(onehot, axis=0)
    rank = jnp.sum(running * onehot, axis=1) - 1
    counts = running[-1]
    padded = (counts + MOE_TILE - 1) // MOE_TILE * MOE_TILE
    pad_ends = jnp.cumsum(padded)
    pad_starts = pad_ends - padded
    slot_of = jnp.sum(onehot * pad_starts[None, :], axis=1) + rank
    assign_of = jnp.full((cap,), -1, jnp.int32).at[slot_of].set(
        jnp.arange(n_assign, dtype=jnp.int32))
    empty = assign_of < 0
    spare = n_assign + jnp.cumsum(empty.astype(jnp.int32)) - 1
    dst = jnp.where(empty, spare, assign_of)
    src = jnp.where(empty, 0, assign_of % n)
    lead = cap + jnp.arange(MOE_TILE, dtype=jnp.int32)
    dst_rows = jnp.concatenate([lead, dst]).astype(jnp.int32)
    src_rows = jnp.concatenate([src, jnp.zeros((MOE_TILE,), jnp.int32)]).astype(jnp.int32)
    block_start = jnp.arange(n_blocks, dtype=jnp.int32) * MOE_TILE
    block_expert = jnp.minimum(
        jnp.sum((block_start[:, None] >= pad_ends[None, :]).astype(jnp.int32), axis=1),
        N_EXPERTS - 1).astype(jnp.int32)

    y = _expert_ffn(h, block_expert, src_rows, dst_rows, wg, wu, wd)
    return _combine(x, y, route, final_g, final)


def _final_norm_kernel(x_ref, g_ref, o_ref):
    o_ref[...] = _rms(x_ref[...], g_ref[...])


def _final_norm(x, g):
    n = x.shape[0]
    rows = pl.BlockSpec((ROW_TILE, D_MODEL), lambda i: (i, 0))
    return pl.pallas_call(
        _final_norm_kernel,
        grid=(n // ROW_TILE,),
        in_specs=[rows, pl.BlockSpec((1, D_MODEL), lambda i: (0, 0))],
        out_specs=rows,
        out_shape=jax.ShapeDtypeStruct((n, D_MODEL), F32),
        compiler_params=_params("parallel"),
        name="final_norm",
    )(x, g)


def kernel(x, w_in, w_out, attn_norm, ffn_norm, sb_out_norm, diff_subln, lambda_q1, lambda_k1,
           lambda_q2, lambda_k2, rel_bias, dense_w_gate, dense_w_up, dense_w_down, router_w,
           expert_w_gate, expert_w_up, expert_w_down, final_norm):
    b, s, d = x.shape
    depth = w_in.shape[0]
    n = b * s
    assert d == D_MODEL and s % DIFF_Q_TILE == 0 and s % ROW_TILE == 0
    assert (n * TOP_K) % MOE_TILE == 0

    scale = HEAD_DIM ** -0.5
    sb_q, sb_k, sb_v = 0, SB_WIDTH, 2 * SB_WIDTH
    df_q, df_k, df_v = 3 * SB_WIDTH, 3 * SB_WIDTH + DIFF_WIDTH, 3 * SB_WIDTH + 2 * DIFF_WIDTH
    bias_tiles = _bias_tiles(rel_bias)
    row2 = lambda v: v.astype(F32).reshape(1, -1)

    xf = x.reshape(n, d)
    for i in range(depth):
        last = i == depth - 1
        w = w_in[i]
        wqk = jnp.concatenate([w[:, sb_q:sb_k] * (scale * LOG2_E), w[:, sb_k:sb_v],
                               w[:, df_q:df_k] * (scale * LOG2_E), w[:, df_k:df_v]],
                              axis=1).astype(BF16)
        wvt = jnp.concatenate([w[:, sb_v:df_q], w[:, df_v:]], axis=1).T.astype(BF16)
        qk, vt = _norm_proj(xf, row2(attn_norm[i]), wqk, wvt, b, s)
        qk = qk.reshape(b, s, QK_WIDTH)
        sb = _sb_attention(qk, vt, row2(jnp.tile(sb_out_norm[i], 2)))
        lambda_init = 0.8 - 0.6 * math.exp(-0.3 * i)
        lam_params = jnp.stack([lambda_q1[i], lambda_k1[i], lambda_q2[i], lambda_k2[i]]).astype(F32)
        df = _diff_attention(qk, vt, bias_tiles, lam_params, row2(diff_subln[i]), lambda_init)
        sb = sb.reshape(n, SB_WIDTH)
        df = df.reshape(n, DIFF_WIDTH)
        wo = w_out[i].astype(BF16)
        j = i // 2
        if i % 2 == 0:
            xf, h = _out_proj(xf, sb, df, wo, row2(ffn_norm[i]))
            xf = _dense_ffn(xf, h, dense_w_gate[j].astype(BF16), dense_w_up[j].astype(BF16),
                            dense_w_down[j].astype(BF16))
            if last:
                xf = _final_norm(xf, row2(final_norm))
        else:
            rw = jnp.pad(router_w[j].astype(F32), ((0, 0), (0, LANES - N_EXPERTS)))
            rw_hi = rw.astype(BF16)
            rw_lo = (rw - rw_hi.astype(F32)).astype(BF16)
            xf, h, route = _out_proj(xf, sb, df, wo, row2(ffn_norm[i]),
                                     jnp.concatenate([rw_hi, rw_lo], axis=1))
            xf = _moe(xf, h, route, expert_w_gate[j].astype(BF16), expert_w_up[j].astype(BF16),
                      expert_w_down[j].astype(BF16), row2(final_norm), last)
    return xf.reshape(b, s, d)
```

```python
import functools
import math

import jax
import jax.numpy as jnp
from jax import lax
from jax.experimental import pallas as pl
from jax.experimental.pallas import tpu as pltpu

D_MODEL = 1024
HEAD_DIM = 64
SB_HEADS = 8
DIFF_HEADS = 4
SB_WIDTH = SB_HEADS * HEAD_DIM
DIFF_WIDTH = DIFF_HEADS * 2 * HEAD_DIM
QK_WIDTH = 2 * SB_WIDTH + 2 * DIFF_WIDTH
V_WIDTH = SB_WIDTH + DIFF_WIDTH
N_BUCKETS = 32
MAX_DISTANCE = 128
N_EXPERTS = 8
TOP_K = 2
NORM_EPS = 1e-6

LANES = 128
MXU_WIDTH = 256
SPARE_ROWS = 8
VMEM_LIMIT = 56 * 1024 * 1024

ROW_TILE = 512
ATT_TILE = 256
DIFF_Q_TILE = 512
DIFF_NEAR_TILES = 3
DIFF_GROUP = 4
SB_GROUP = 4
MOE_TILE = 512
MOE_FF_TILE = 1792
MOE_FF_STEPS = 2
SB_LOG_WEIGHT_FLOOR = -150.0
MASKED_LOGIT = -1e30
LOG2_E = math.log2(math.e)
DIFF_SUM_ROWS = 16

F32 = jnp.float32
BF16 = jnp.bfloat16
NT_DIMS = (((1,), (1,)), ((), ()))


def _params(*semantics):
    return pltpu.CompilerParams(dimension_semantics=semantics, vmem_limit_bytes=VMEM_LIMIT)


def _rms(x, g):
    return x * lax.rsqrt(jnp.mean(x * x, axis=-1, keepdims=True) + NORM_EPS) * g


def _norm_proj_kernel(x_ref, g_ref, wqk_ref, wvt_ref, qk_ref, vt_ref):
    h = _rms(x_ref[...], g_ref[...]).astype(BF16)
    for c in range(QK_WIDTH // D_MODEL):
        cols = slice(c * D_MODEL, (c + 1) * D_MODEL)
        qk_ref[:, cols] = jnp.dot(h, wqk_ref[:, cols], preferred_element_type=F32).astype(BF16)
    vt_ref[...] = lax.dot_general(wvt_ref[...], h, NT_DIMS,
                                  preferred_element_type=F32).astype(BF16)


def _norm_proj(x, g, wqk, wvt, batch, seq):
    n = x.shape[0]
    per_seq = seq // ROW_TILE
    whole = lambda a: pl.BlockSpec(a.shape, lambda b, i: (0, 0), pipeline_mode=pl.Buffered(1))
    return pl.pallas_call(
        _norm_proj_kernel,
        grid=(batch, per_seq),
        in_specs=[
            pl.BlockSpec((ROW_TILE, D_MODEL), lambda b, i: (b * per_seq + i, 0)),
            pl.BlockSpec((1, D_MODEL), lambda b, i: (0, 0)),
            whole(wqk), whole(wvt),
        ],
        out_specs=[
            pl.BlockSpec((ROW_TILE, QK_WIDTH), lambda b, i: (b * per_seq + i, 0)),
            pl.BlockSpec((None, V_WIDTH, ROW_TILE), lambda b, i: (b, 0, i)),
        ],
        out_shape=[jax.ShapeDtypeStruct((n, QK_WIDTH), BF16),
                   jax.ShapeDtypeStruct((batch, V_WIDTH, seq), BF16)],
        compiler_params=_params("parallel", "parallel"),
        name="norm_proj",
    )(x, g, wqk, wvt)


def _sb_kernel(q_ref, k_ref, vt_ref, g_ref, o_ref, *scratch):
    t = ATT_TILE
    i = pl.program_id(1)
    pairs = range(SB_GROUP)
    qq = scratch[:SB_GROUP]
    acc = scratch[SB_GROUP:]
    lane = lax.broadcasted_iota(jnp.int32, (1, LANES), 1)
    for p in pairs:
        q = q_ref[:, p * LANES:(p + 1) * LANES]
        qq[p][0:t, :] = jnp.where(lane < HEAD_DIM, q, jnp.zeros_like(q))
        qq[p][t:2 * t, :] = jnp.where(lane < HEAD_DIM, jnp.zeros_like(q), q)
    key = lax.broadcasted_iota(jnp.int32, (t, t), 0)
    pos = lax.broadcasted_iota(jnp.int32, (t, t), 1)
    suffix_ones = (pos >= key).astype(BF16)
    key2 = lax.broadcasted_iota(jnp.int32, (t, 2 * t), 0)
    qry2 = lax.broadcasted_iota(jnp.int32, (t, 2 * t), 1)
    strictly_causal = key2 < jnp.where(qry2 >= t, qry2 - t, qry2)

    def scores(p, j):
        start = pl.multiple_of(j * t, t)
        return lax.dot_general(k_ref[pl.ds(start, t), p * LANES:(p + 1) * LANES], qq[p][...],
                               NT_DIMS, preferred_element_type=F32)

    def split_log_stay(z, diagonal):
        neg_z = -z
        log_stay = jnp.minimum(neg_z, 0.0) - jnp.log2(1.0 + jnp.exp2(jnp.minimum(z, neg_z)))
        if diagonal:
            log_stay = jnp.where(strictly_causal, log_stay, 0.0)
        hi = log_stay.astype(BF16)
        return hi, (log_stay - hi.astype(F32)).astype(BF16)

    def suffix_sums(hi, lo):
        return (jnp.dot(suffix_ones, hi, preferred_element_type=F32)
                + jnp.dot(suffix_ones, lo, preferred_element_type=F32))

    def weights(z, incl, carry, diagonal):
        log_a = z + incl
        if carry is not None:
            log_a = log_a + carry
        a = jnp.exp2(log_a)
        if diagonal:
            a = jnp.where(strictly_causal, a, 0.0)
        return a.astype(BF16), incl[0:1, :] if carry is None else carry + incl[0:1, :]

    def accumulate(p, j, a, first):
        start = pl.multiple_of(j * t, t)
        update = jnp.dot(vt_ref[p * LANES:(p + 1) * LANES, pl.ds(start, t)], a,
                         preferred_element_type=F32)
        if first:
            acc[p][...] = update
        else:
            acc[p][...] += update

    def tile(p, j, carry, diagonal):
        z = scores(p, j)
        a, carry = weights(z, suffix_sums(*split_log_stay(z, diagonal)), carry, diagonal)
        accumulate(p, j, a, diagonal)
        return carry

    @pl.when(i == 0)
    def _():
        for p in pairs:
            tile(p, i, None, True)

    @pl.when(i > 0)
    def _():
        z_d = [scores(p, i) for p in pairs]
        z_p = [scores(p, i - 1) for p in pairs]
        split_d = [split_log_stay(z, True) for z in z_d]
        incl_d = [suffix_sums(*sp) for sp in split_d]
        split_p = [split_log_stay(z, False) for z in z_p]
        first = [weights(z_d[p], incl_d[p], None, True) for p in pairs]
        incl_p = [suffix_sums(*sp) for sp in split_p]
        for p in pairs:
            accumulate(p, i, first[p][0], True)
        second = [weights(z_p[p], incl_p[p], first[p][1], False) for p in pairs]
        for p in pairs:
            accumulate(p, i - 1, second[p][0], False)
        carries = tuple(second[p][1] for p in pairs)

        def top_of(carries):
            return functools.reduce(jnp.maximum, [jnp.max(c) for c in carries])

        def cond(state):
            j, _, top_carry = state
            return (j >= 0) & (top_carry > SB_LOG_WEIGHT_FLOOR)

        def body(state):
            j, carries, _ = state
            carries = tuple(tile(p, j, carries[p], False) for p in pairs)
            return j - 1, carries, top_of(carries)

        lax.while_loop(cond, body, (i - 2, carries, top_of(carries)))

    for p in pairs:
        normed = []
        for h in range(2):
            o = acc[p][h * HEAD_DIM:(h + 1) * HEAD_DIM, h * t:(h + 1) * t]
            ms = jnp.mean(o * o, axis=0, keepdims=True)
            normed.append(o * lax.rsqrt(ms + NORM_EPS))
        o_ref[:, p * LANES:(p + 1) * LANES] = (
            jnp.concatenate(normed, axis=0).T * g_ref[...]).astype(o_ref.dtype)


def _sb_attention(qk, vt, g2):
    b, s, _ = qk.shape
    t = ATT_TILE
    width = SB_GROUP * LANES
    assert width == SB_WIDTH
    return pl.pallas_call(
        _sb_kernel,
        grid=(b, s // t),
        in_specs=[
            pl.BlockSpec((None, t, width), lambda bi, i: (bi, i, 0)),
            pl.BlockSpec((None, s, width), lambda bi, i: (bi, 0, 1), pipeline_mode=pl.Buffered(1)),
            pl.BlockSpec((None, width, s), lambda bi, i: (bi, 0, 0), pipeline_mode=pl.Buffered(1)),
            pl.BlockSpec((1, LANES), lambda bi, i: (0, 0)),
        ],
        out_specs=pl.BlockSpec((None, t, width), lambda bi, i: (bi, i, 0)),
        out_shape=jax.ShapeDtypeStruct((b, s, SB_WIDTH), BF16),
        scratch_shapes=([pltpu.VMEM((2 * t, LANES), BF16)] * SB_GROUP
                        + [pltpu.VMEM((LANES, 2 * t), F32)] * SB_GROUP),
        compiler_params=_params("parallel", "arbitrary"),
        name="sb_attention",
    )(qk, qk, vt, g2)


def _diff_kernel(q_ref, k_ref, vt_ref, bias_ref, lam_ref, g_ref, o_ref, *scratch, lambda_init):
    t = ATT_TILE
    tq = DIFF_Q_TILE
    i = pl.program_id(2)
    n_tiles = (tq // t) * (i + 1)
    heads = range(DIFF_GROUP)
    per_head = len(scratch) // DIFF_GROUP
    bufs = [scratch[h * per_head:(h + 1) * per_head] for h in heads]
    qq = [b[0] for b in bufs]
    even = [b[1:5] for b in bufs]
    odd = [b[5:9] for b in bufs]
    m_refs = [b[9] for b in bufs]
    acc_refs = [b[10] for b in bufs]
    lane = lax.broadcasted_iota(jnp.int32, (1, LANES), 1)
    ones_rows = jnp.ones((DIFF_SUM_ROWS, t), BF16)
    for h in heads:
        q = q_ref[:, h * LANES:(h + 1) * LANES]
        qq[h][0:tq, :] = jnp.where(lane < HEAD_DIM, q, jnp.zeros_like(q))
        qq[h][tq:2 * tq, :] = jnp.where(lane < HEAD_DIM, jnp.zeros_like(q), q)
        m_refs[h][...] = jnp.full_like(m_refs[h], MASKED_LOGIT)
        acc_refs[h][...] = jnp.zeros_like(acc_refs[h])

    def key_start(n):
        return pl.multiple_of((n_tiles - 1 - n) * t, t)

    def scores(n, parity, near=None):
        for h in heads:
            z_ref, zmax_ref, _, _ = parity[h]
            z = lax.dot_general(k_ref[pl.ds(key_start(n), t), h * LANES:(h + 1) * LANES],
                                qq[h][...], NT_DIMS, preferred_element_type=F32)
            if near is not None:
                z = z + bias_ref[h, near]
            z_ref[...] = z
            zmax_ref[...] = jnp.max(z, axis=0, keepdims=True)

    def softmax(parity):
        for h in heads:
            z_ref, zmax_ref, p_ref, a_ref = parity[h]
            m_prev = m_refs[h][...]
            m_new = jnp.maximum(m_prev, zmax_ref[...])
            a_ref[...] = jnp.exp2(m_prev - m_new)
            p_ref[...] = jnp.exp2(z_ref[...] - m_new).astype(BF16)
            m_refs[h][...] = m_new

    def values(n, parity):
        for h in heads:
            _, _, p_ref, a_ref = parity[h]
            vt = jnp.concatenate([vt_ref[h * LANES:(h + 1) * LANES, pl.ds(key_start(n), t)],
                                  ones_rows], axis=0)
            acc_refs[h][...] = a_ref[...] * acc_refs[h][...] + jnp.dot(
                vt, p_ref[...], preferred_element_type=F32)

    scores(0, even, 0)
    scores(1, odd, 1)
    softmax(even)

    @pl.when(i > 0)
    def _():
        scores(2, even, 2)
        softmax(odd)
        values(0, even)
        scores(3, odd)
        softmax(even)
        values(1, odd)

        def pair(k, carry):
            n = 2 * k + 2
            scores(n, even)
            softmax(odd)
            values(n - 2, even)
            scores(n + 1, odd)
            softmax(even)
            values(n - 1, odd)
            return carry

        lax.fori_loop(1, i, pair, 0)

    softmax(odd)
    values(n_tiles - 2, even)
    values(n_tiles - 1, odd)

    lp = lam_ref[...]
    lam = (jnp.exp(jnp.sum(lp[0:1] * lp[1:2], axis=-1, keepdims=True))
           - jnp.exp(jnp.sum(lp[2:3] * lp[3:4], axis=-1, keepdims=True)) + lambda_init)
    for h in heads:
        acc_ref = acc_refs[h]
        normalised = acc_ref[0:LANES, :] / acc_ref[LANES:LANES + 1, :]
        o = normalised[:, :tq] - lam * normalised[:, tq:]
        o = o * lax.rsqrt(jnp.mean(o * o, axis=0, keepdims=True) + NORM_EPS)
        o_ref[:, h * LANES:(h + 1) * LANES] = (
            o.T * g_ref[...] * (1.0 - lambda_init)).astype(o_ref.dtype)


def _diff_attention(qk, vt, bias_tiles, lam_params, g, lambda_init):
    b, s, _ = qk.shape
    t = ATT_TILE
    tq = DIFF_Q_TILE
    width = DIFF_GROUP * LANES
    q0 = 2 * SB_WIDTH // width
    v0 = SB_WIDTH // width
    groups = DIFF_HEADS // DIFF_GROUP
    tile_f32 = pltpu.VMEM((t, 2 * tq), F32)
    tile_bf16 = pltpu.VMEM((t, 2 * tq), BF16)
    row_f32 = pltpu.VMEM((1, 2 * tq), F32)
    stage = [tile_f32, row_f32, tile_bf16, row_f32]
    per_head = ([pltpu.VMEM((2 * tq, LANES), BF16)] + stage + stage
                + [row_f32, pltpu.VMEM((LANES + DIFF_SUM_ROWS, 2 * tq), F32)])
    return pl.pallas_call(
        functools.partial(_diff_kernel, lambda_init=lambda_init),
        grid=(b, groups, s // tq),
        in_specs=[
            pl.BlockSpec((None, tq, width), lambda bi, hg, i: (bi, i, q0 + hg)),
            pl.BlockSpec((None, s, width), lambda bi, hg, i: (bi, 0, q0 + groups + hg),
                         pipeline_mode=pl.Buffered(1)),
            pl.BlockSpec((None, width, s), lambda bi, hg, i: (bi, v0 + hg, 0),
                         pipeline_mode=pl.Buffered(1)),
            pl.BlockSpec((DIFF_GROUP, DIFF_NEAR_TILES, t, 2 * tq), lambda bi, hg, i: (hg, 0, 0, 0),
                         pipeline_mode=pl.Buffered(1)),
            pl.BlockSpec((4, HEAD_DIM), lambda bi, hg, i: (0, 0)),
            pl.BlockSpec((1, LANES), lambda bi, hg, i: (0, 0)),
        ],
        out_specs=pl.BlockSpec((None, tq, width), lambda bi, hg, i: (bi, i, hg)),
        out_shape=jax.ShapeDtypeStruct((b, s, DIFF_WIDTH), BF16),
        scratch_shapes=per_head * DIFF_GROUP,
        compiler_params=_params("parallel", "parallel", "arbitrary"),
        name="diff_attention",
    )(qk, qk, vt, bias_tiles, lam_params, g)


def _t5_causal_bucket(n):
    max_exact = N_BUCKETS // 2
    nf = jnp.maximum(n, 1).astype(F32)
    large = max_exact + (jnp.log(nf / max_exact) / math.log(MAX_DISTANCE / max_exact)
                         * (N_BUCKETS - max_exact)).astype(jnp.int32)
    large = jnp.minimum(large, N_BUCKETS - 1)
    return jnp.where(n < max_exact, n, large)


def _bias_tiles(rel_bias):
    t = ATT_TILE
    assert MAX_DISTANCE <= t and DIFF_NEAR_TILES == DIFF_Q_TILE // t + 1
    key = jnp.arange(t, dtype=jnp.int32)[:, None]
    qry = jnp.arange(DIFF_Q_TILE, dtype=jnp.int32)[None, :]
    dist = jnp.stack([qry - key - (DIFF_Q_TILE - (n + 1) * t) for n in range(DIFF_NEAR_TILES)])
    bucket = _t5_causal_bucket(jnp.maximum(dist, 0))
    table = (rel_bias.astype(F32) - rel_bias.astype(F32)[N_BUCKETS - 1][None, :]) * LOG2_E
    near = jnp.zeros((DIFF_HEADS,) + dist.shape, F32)
    for k in range(N_BUCKETS - 1):
        near = jnp.where(bucket[None] == k, table[k][:, None, None, None], near)
    near = jnp.where((dist >= 0)[None], near, MASKED_LOGIT)
    return jnp.concatenate([near, near], axis=-1)


def _route_top2(h, rw_ref):
    h_hi = h.astype(BF16)
    h_lo = (h - h_hi.astype(F32)).astype(BF16)
    both = jnp.dot(h_hi, rw_ref[...], preferred_element_type=F32)
    logits = (both[:, :LANES] + both[:, LANES:]
              + jnp.dot(h_lo, rw_ref[:, :LANES], preferred_element_type=F32))
    lane = lax.broadcasted_iota(jnp.int32, logits.shape, 1)
    neg = -jnp.inf
    lg = jnp.where(lane < N_EXPERTS, logits, neg)
    m1 = jnp.max(lg, axis=-1, keepdims=True)
    i1 = jnp.min(jnp.where(lg == m1, lane, LANES), axis=-1, keepdims=True)
    lg2 = jnp.where(lane == i1, neg, lg)
    m2 = jnp.max(lg2, axis=-1, keepdims=True)
    i2 = jnp.min(jnp.where(lg2 == m2, lane, LANES), axis=-1, keepdims=True)
    e = jnp.exp(m2 - m1)
    g1 = 1.0 / (1.0 + e)
    g2 = e * g1
    return jnp.where(lane == 0, i1.astype(F32),
                     jnp.where(lane == 1, i2.astype(F32),
                               jnp.where(lane == 2, g1, jnp.where(lane == 3, g2, 0.0))))


def _out_proj_route_kernel(x_ref, sb_ref, df_ref, wo_ref, g_ref, rw_ref, xo_ref, h_ref, route_ref):
    y = (jnp.dot(sb_ref[...], wo_ref[:SB_WIDTH, :], preferred_element_type=F32)
         + jnp.dot(df_ref[...], wo_ref[SB_WIDTH:, :], preferred_element_type=F32))
    xn = x_ref[...] + y
    xo_ref[...] = xn
    h = _rms(xn, g_ref[...])
    h_ref[...] = h
    route_ref[...] = _route_top2(h, rw_ref)


def _out_proj_route(x, sb, df, wo, g, router_w):
    n = x.shape[0]
    rows = lambda w: pl.BlockSpec((ROW_TILE, w), lambda i: (i, 0))
    whole = lambda a: pl.BlockSpec(a.shape, lambda i: (0, 0))
    return pl.pallas_call(
        _out_proj_route_kernel,
        grid=(n // ROW_TILE,),
        in_specs=[rows(D_MODEL), rows(SB_WIDTH), rows(DIFF_WIDTH), whole(wo), whole(g),
                  whole(router_w)],
        out_specs=[rows(D_MODEL), rows(D_MODEL), rows(LANES)],
        out_shape=[jax.ShapeDtypeStruct((n, D_MODEL), F32), jax.ShapeDtypeStruct((n, D_MODEL), F32),
                   jax.ShapeDtypeStruct((n, LANES), F32)],
        compiler_params=_params("parallel"),
        name="out_proj_route",
    )(x, sb, df, wo, g, router_w)


def _dense_ffn_kernel(x_ref, sb_ref, df_ref, wo_ref, g_ref, wg_ref, wu_ref, wd_ref, o_ref):
    y = (jnp.dot(sb_ref[...], wo_ref[:SB_WIDTH, :], preferred_element_type=F32)
         + jnp.dot(df_ref[...], wo_ref[SB_WIDTH:, :], preferred_element_type=F32))
    xn = x_ref[...] + y
    h = _rms(xn, g_ref[...]).astype(BF16)
    gate = jnp.dot(h, wg_ref[...], preferred_element_type=F32)
    up = jnp.dot(h, wu_ref[...], preferred_element_type=F32)
    act = (gate * jax.nn.sigmoid(gate) * up).astype(BF16)
    o_ref[...] = xn + jnp.dot(act, wd_ref[...], preferred_element_type=F32)


def _out_proj_dense_ffn(x, sb, df, wo, g, wg, wu, wd):
    n = x.shape[0]
    tm = ROW_TILE // 2
    rows = lambda w: pl.BlockSpec((tm, w), lambda i: (i, 0))
    whole = lambda a: pl.BlockSpec(a.shape, lambda i: (0, 0), pipeline_mode=pl.Buffered(1))
    return pl.pallas_call(
        _dense_ffn_kernel,
        grid=(n // tm,),
        in_specs=[rows(D_MODEL), rows(SB_WIDTH), rows(DIFF_WIDTH), whole(wo), whole(g),
                  whole(wg), whole(wu), whole(wd)],
        out_specs=rows(D_MODEL),
        out_shape=jax.ShapeDtypeStruct((n, D_MODEL), F32),
        compiler_params=_params("parallel"),
        name="out_proj_dense_ffn",
    )(x, sb, df, wo, g, wg, wu, wd)


def _expert_kernel(be_ref, src_ref, dst_ref, h_ref, wg_ref, wu_ref, wd_ref, y_ref,
                   xbuf, xb_ref, obuf, acc_ref, gsem, ssem):
    tm = MOE_TILE
    i = pl.program_id(0)
    f = pl.program_id(1)
    nb = pl.num_programs(0)
    nf = pl.num_programs(1)
    chunk = tm // MOE_FF_STEPS
    slot = i % 2
    other = 1 - slot

    def gather_row(r, row):
        return pltpu.make_async_copy(h_ref.at[pl.ds(row, 1), :], xbuf.at[pl.ds(r, 1), :], gsem)

    def scatter_row(slot_id, r, row):
        return pltpu.make_async_copy(obuf.at[slot_id, pl.ds(r, 1), :], y_ref.at[pl.ds(row, 1), :],
                                     ssem.at[slot_id])

    def wait_gather():
        pltpu.make_async_copy(h_ref.at[pl.ds(0, tm), :], xbuf.at[pl.ds(0, tm), :], gsem).wait()

    def wait_scatter(slot_id):
        pltpu.make_async_copy(obuf.at[slot_id], y_ref.at[pl.ds(0, tm), :], ssem.at[slot_id]).wait()

    @pl.when((i == 0) & (f == 0))
    def _():
        obuf[...] = jnp.zeros_like(obuf)
        xbuf[tm:, :] = jnp.zeros((SPARE_ROWS, D_MODEL), F32)

        def first(r, carry):
            gather_row(r, src_ref[r]).start()
            return carry

        lax.fori_loop(0, tm, first, 0)

    @pl.when(f == 0)
    def _():
        wait_gather()
        xb_ref[...] = xbuf[0:tm, :].astype(BF16)
        acc_ref[...] = jnp.zeros_like(acc_ref)

    x = xb_ref[...]
    n_slices = MOE_FF_TILE // MXU_WIDTH
    done = 0
    anchor = None
    for c in range(n_slices):
        cols = slice(c * MXU_WIDTH, (c + 1) * MXU_WIDTH)
        gate = jnp.dot(x, wg_ref[:, cols], preferred_element_type=F32)
        up = jnp.dot(x, wu_ref[:, cols], preferred_element_type=F32)
        if anchor is not None:
            up = up + anchor
        act = (gate * jax.nn.sigmoid(gate) * up).astype(BF16)
        acc_ref[...] += jnp.dot(act, wd_ref[cols, :], preferred_element_type=F32)
        if c < n_slices - 1:
            upto = chunk * (c + 1) // (n_slices - 1)
            for r in range(done, upto):
                rr = f * chunk + r
                gather_row(rr, src_ref[(i + 1) * tm + rr]).start()
            done = upto
            anchor = jnp.where(i < 0, xbuf[tm:tm + 1, 0:MXU_WIDTH], 0.0)

    for r in range(chunk):
        rr = f * chunk + r
        scatter_row(other, rr, dst_ref[i * tm + rr]).start()

    @pl.when(f == nf - 1)
    def _():
        @pl.when(i >= 1)
        def _():
            wait_scatter(slot)

        obuf[slot] = acc_ref[...]

        @pl.when(i == nb - 1)
        def _():
            def last(r, carry):
                scatter_row(slot, r, dst_ref[(i + 1) * tm + r]).start()
                return carry

            lax.fori_loop(0, tm, last, 0)
            wait_scatter(slot)
            wait_scatter(other)
            wait_gather()


def _expert_ffn(h, block_expert, src_rows, dst_rows, wg, wu, wd):
    n_slots = src_rows.shape[0]
    nb = n_slots // MOE_TILE - 1
    d_ff = wg.shape[-1]
    nf = d_ff // MOE_FF_TILE
    assert nf == MOE_FF_STEPS and MOE_TILE % nf == 0
    return pl.pallas_call(
        _expert_kernel,
        grid_spec=pltpu.PrefetchScalarGridSpec(
            num_scalar_prefetch=3,
            grid=(nb, nf),
            in_specs=[
                pl.BlockSpec(memory_space=pl.ANY),
                pl.BlockSpec((None, D_MODEL, MOE_FF_TILE), lambda i, f, be, s, d: (be[i], 0, f)),
                pl.BlockSpec((None, D_MODEL, MOE_FF_TILE), lambda i, f, be, s, d: (be[i], 0, f)),
                pl.BlockSpec((None, MOE_FF_TILE, D_MODEL), lambda i, f, be, s, d: (be[i], f, 0)),
            ],
            out_specs=pl.BlockSpec(memory_space=pl.ANY),
            scratch_shapes=[pltpu.VMEM((MOE_TILE + SPARE_ROWS, D_MODEL), F32),
                            pltpu.VMEM((MOE_TILE, D_MODEL), BF16),
                            pltpu.VMEM((2, MOE_TILE, D_MODEL), F32),
                            pltpu.VMEM((MOE_TILE, D_MODEL), F32),
                            pltpu.SemaphoreType.DMA(()), pltpu.SemaphoreType.DMA((2,))],
        ),
        out_shape=jax.ShapeDtypeStruct((n_slots, D_MODEL), F32),
        compiler_params=_params("arbitrary", "arbitrary"),
        name="expert_ffn",
    )(block_expert, src_rows, dst_rows, h, wg, wu, wd)


def _combine_kernel(x_ref, y0_ref, y1_ref, route_ref, g_ref, o_ref, *, final):
    r = route_ref[...]
    out = x_ref[...] + r[:, 2:3] * y0_ref[...] + r[:, 3:4] * y1_ref[...]
    if final:
        out = _rms(out, g_ref[...])
    o_ref[...] = out


def _combine(x, y, route, g, final):
    n = x.shape[0]
    second = n // ROW_TILE
    rows = lambda w: pl.BlockSpec((ROW_TILE, w), lambda i: (i, 0))
    return pl.pallas_call(
        functools.partial(_combine_kernel, final=final),
        grid=(n // ROW_TILE,),
        in_specs=[rows(D_MODEL), rows(D_MODEL),
                  pl.BlockSpec((ROW_TILE, D_MODEL), lambda i: (second + i, 0)),
                  rows(LANES), pl.BlockSpec((1, D_MODEL), lambda i: (0, 0))],
        out_specs=rows(D_MODEL),
        out_shape=jax.ShapeDtypeStruct((n, D_MODEL), F32),
        compiler_params=_params("parallel"),
        name="moe_combine",
    )(x, y, y, route, g)


def _moe(x, h, route, wg, wu, wd, final_g, final):
    n = x.shape[0]
    n_assign = n * TOP_K
    n_blocks = n_assign // MOE_TILE + N_EXPERTS
    cap = n_blocks * MOE_TILE
    experts = route[:, :TOP_K].astype(jnp.int32).T.reshape(n_assign)
    onehot = (experts[:, None] == jnp.arange(N_EXPERTS, dtype=jnp.int32)[None, :]).astype(jnp.int32)
    running = jnp.cumsum(onehot, axis=0)
    rank = jnp.sum(running * onehot, axis=1) - 1
    counts = running[-1]
    padded = (counts + MOE_TILE - 1) // MOE_TILE * MOE_TILE
    pad_ends = jnp.cumsum(padded)
    pad_starts = pad_ends - padded
    slot_of = jnp.sum(onehot * pad_starts[None, :], axis=1) + rank
    assign_of = jnp.full((cap,), -1, jnp.int32).at[slot_of].set(
        jnp.arange(n_assign, dtype=jnp.int32))
    empty = assign_of < 0
    spare = n_assign + jnp.cumsum(empty.astype(jnp.int32)) - 1
    dst = jnp.where(empty, spare, assign_of)
    src = jnp.where(empty, 0, assign_of % n)
    lead = cap + jnp.arange(MOE_TILE, dtype=jnp.int32)
    dst_rows = jnp.concatenate([lead, dst]).astype(jnp.int32)
    src_rows = jnp.concatenate([src, jnp.zeros((MOE_TILE,), jnp.int32)]).astype(jnp.int32)
    block_start = jnp.arange(n_blocks, dtype=jnp.int32) * MOE_TILE
    block_expert = jnp.minimum(
        jnp.sum((block_start[:, None] >= pad_ends[None, :]).astype(jnp.int32), axis=1),
        N_EXPERTS - 1).astype(jnp.int32)

    y = _expert_ffn(h, block_expert, src_rows, dst_rows, wg, wu, wd)
    return _combine(x, y, route, final_g, final)


def _final_norm_kernel(x_ref, g_ref, o_ref):
    o_ref[...] = _rms(x_ref[...], g_ref[...])


def _final_norm(x, g):
    n = x.shape[0]
    rows = pl.BlockSpec((ROW_TILE, D_MODEL), lambda i: (i, 0))
    return pl.pallas_call(
        _final_norm_kernel,
        grid=(n // ROW_TILE,),
        in_specs=[rows, pl.BlockSpec((1, D_MODEL), lambda i: (0, 0))],
        out_specs=rows,
        out_shape=jax.ShapeDtypeStruct((n, D_MODEL), F32),
        compiler_params=_params("parallel"),
        name="final_norm",
    )(x, g)


def kernel(x, w_in, w_out, attn_norm, ffn_norm, sb_out_norm, diff_subln, lambda_q1, lambda_k1,
           lambda_q2, lambda_k2, rel_bias, dense_w_gate, dense_w_up, dense_w_down, router_w,
           expert_w_gate, expert_w_up, expert_w_down, final_norm):
    b, s, d = x.shape
    depth = w_in.shape[0]
    n = b * s
    assert d == D_MODEL and s % DIFF_Q_TILE == 0 and s % ROW_TILE == 0
    assert (n * TOP_K) % MOE_TILE == 0

    scale = HEAD_DIM ** -0.5
    sb_q, sb_k, sb_v = 0, SB_WIDTH, 2 * SB_WIDTH
    df_q, df_k, df_v = 3 * SB_WIDTH, 3 * SB_WIDTH + DIFF_WIDTH, 3 * SB_WIDTH + 2 * DIFF_WIDTH
    bias_tiles = _bias_tiles(rel_bias)
    row2 = lambda v: v.astype(F32).reshape(1, -1)

    xf = x.reshape(n, d)
    for i in range(depth):
        last = i == depth - 1
        w = w_in[i]
        wqk = jnp.concatenate([w[:, sb_q:sb_k] * (scale * LOG2_E), w[:, sb_k:sb_v],
                               w[:, df_q:df_k] * (scale * LOG2_E), w[:, df_k:df_v]],
                              axis=1).astype(BF16)
        wvt = jnp.concatenate([w[:, sb_v:df_q], w[:, df_v:]], axis=1).T.astype(BF16)
        qk, vt = _norm_proj(xf, row2(attn_norm[i]), wqk, wvt, b, s)
        qk = qk.reshape(b, s, QK_WIDTH)
        sb = _sb_attention(qk, vt, row2(jnp.tile(sb_out_norm[i], 2)))
        lambda_init = 0.8 - 0.6 * math.exp(-0.3 * i)
        lam_params = jnp.stack([lambda_q1[i], lambda_k1[i], lambda_q2[i], lambda_k2[i]]).astype(F32)
        df = _diff_attention(qk, vt, bias_tiles, lam_params, row2(diff_subln[i]), lambda_init)
        sb = sb.reshape(n, SB_WIDTH)
        df = df.reshape(n, DIFF_WIDTH)
        wo = w_out[i].astype(BF16)
        j = i // 2
        if i % 2 == 0:
            xf = _out_proj_dense_ffn(xf, sb, df, wo, row2(ffn_norm[i]),
                                     dense_w_gate[j].astype(BF16), dense_w_up[j].astype(BF16),
                                     dense_w_down[j].astype(BF16))
            if last:
                xf = _final_norm(xf, row2(final_norm))
        else:
            rw = jnp.pad(router_w[j].astype(F32), ((0, 0), (0, LANES - N_EXPERTS)))
            rw_hi = rw.astype(BF16)
            rw_lo = (rw - rw_hi.astype(F32)).astype(BF16)
            xf, h, route = _out_proj_route(xf, sb, df, wo, row2(ffn_norm[i]),
                                           jnp.concatenate([rw_hi, rw_lo], axis=1))
            xf = _moe(xf, h, route, expert_w_gate[j].astype(BF16), expert_w_up[j].astype(BF16),
                      expert_w_down[j].astype(BF16), row2(final_norm), last)
    return xf.reshape(b, s, d)
```

```python
import functools
import math

import jax
import jax.numpy as jnp
from jax import lax
from jax.experimental import pallas as pl
from jax.experimental.pallas import tpu as pltpu

D_MODEL = 1024
HEAD_DIM = 64
SB_HEADS = 8
DIFF_HEADS = 4
SB_WIDTH = SB_HEADS * HEAD_DIM
DIFF_WIDTH = DIFF_HEADS * 2 * HEAD_DIM
QK_WIDTH = 2 * SB_WIDTH + 2 * DIFF_WIDTH
V_WIDTH = SB_WIDTH + DIFF_WIDTH
N_BUCKETS = 32
MAX_DISTANCE = 128
N_EXPERTS = 8
TOP_K = 2
NORM_EPS = 1e-6

LANES = 128
MXU_WIDTH = 256
SPARE_ROWS = 8
VMEM_LIMIT = 56 * 1024 * 1024

ROW_TILE = 512
ATT_TILE = 256
DIFF_Q_TILE = 512
DIFF_NEAR_TILES = 3
DIFF_GROUP = 4
SB_GROUP = 4
MOE_TILE = 512
MOE_FF_TILE = 1792
MOE_FF_STEPS = 2
SB_LOG_WEIGHT_FLOOR = -150.0
MASKED_LOGIT = -1e30
LOG2_E = math.log2(math.e)
DIFF_SUM_ROWS = 16

F32 = jnp.float32
BF16 = jnp.bfloat16
NT_DIMS = (((1,), (1,)), ((), ()))


def _params(*semantics):
    return pltpu.CompilerParams(dimension_semantics=semantics, vmem_limit_bytes=VMEM_LIMIT)


def _rms(x, g):
    return x * lax.rsqrt(jnp.mean(x * x, axis=-1, keepdims=True) + NORM_EPS) * g


def _norm_proj_kernel(x_ref, g_ref, wqk_ref, wvt_ref, qk_ref, vt_ref):
    h = _rms(x_ref[...], g_ref[...]).astype(BF16)
    for c in range(QK_WIDTH // D_MODEL):
        cols = slice(c * D_MODEL, (c + 1) * D_MODEL)
        qk_ref[:, cols] = jnp.dot(h, wqk_ref[:, cols], preferred_element_type=F32).astype(BF16)
    vt_ref[...] = lax.dot_general(wvt_ref[...], h, NT_DIMS,
                                  preferred_element_type=F32).astype(BF16)


def _norm_proj(x, g, wqk, wvt, batch, seq):
    n = x.shape[0]
    per_seq = seq // ROW_TILE
    whole = lambda a: pl.BlockSpec(a.shape, lambda b, i: (0, 0), pipeline_mode=pl.Buffered(1))
    return pl.pallas_call(
        _norm_proj_kernel,
        grid=(batch, per_seq),
        in_specs=[
            pl.BlockSpec((ROW_TILE, D_MODEL), lambda b, i: (b * per_seq + i, 0)),
            pl.BlockSpec((1, D_MODEL), lambda b, i: (0, 0)),
            whole(wqk), whole(wvt),
        ],
        out_specs=[
            pl.BlockSpec((ROW_TILE, QK_WIDTH), lambda b, i: (b * per_seq + i, 0)),
            pl.BlockSpec((None, V_WIDTH, ROW_TILE), lambda b, i: (b, 0, i)),
        ],
        out_shape=[jax.ShapeDtypeStruct((n, QK_WIDTH), BF16),
                   jax.ShapeDtypeStruct((batch, V_WIDTH, seq), BF16)],
        compiler_params=_params("parallel", "parallel"),
        name="norm_proj",
    )(x, g, wqk, wvt)


def _sb_kernel(q_ref, k_ref, vt_ref, g_ref, o_ref, *scratch):
    t = ATT_TILE
    i = pl.program_id(1)
    pairs = range(SB_GROUP)
    qq = scratch[:SB_GROUP]
    acc = scratch[SB_GROUP:]
    lane = lax.broadcasted_iota(jnp.int32, (1, LANES), 1)
    for p in pairs:
        q = q_ref[:, p * LANES:(p + 1) * LANES]
        qq[p][0:t, :] = jnp.where(lane < HEAD_DIM, q, jnp.zeros_like(q))
        qq[p][t:2 * t, :] = jnp.where(lane < HEAD_DIM, jnp.zeros_like(q), q)
    key = lax.broadcasted_iota(jnp.int32, (t, t), 0)
    pos = lax.broadcasted_iota(jnp.int32, (t, t), 1)
    suffix_ones = (pos >= key).astype(BF16)
    key2 = lax.broadcasted_iota(jnp.int32, (t, 2 * t), 0)
    qry2 = lax.broadcasted_iota(jnp.int32, (t, 2 * t), 1)
    strictly_causal = key2 < jnp.where(qry2 >= t, qry2 - t, qry2)

    def scores(p, j, diagonal):
        start = pl.multiple_of(j * t, t)
        z = lax.dot_general(k_ref[pl.ds(start, t), p * LANES:(p + 1) * LANES], qq[p][...],
                            NT_DIMS, preferred_element_type=F32)
        return jnp.where(strictly_causal, z, MASKED_LOGIT) if diagonal else z

    def split_log_stay(z):
        neg_z = -z
        log_stay = jnp.minimum(neg_z, 0.0) - jnp.log2(1.0 + jnp.exp2(jnp.minimum(z, neg_z)))
        hi = log_stay.astype(BF16)
        return hi, (log_stay - hi.astype(F32)).astype(BF16)

    def suffix_sums(hi, lo):
        return (jnp.dot(suffix_ones, hi, preferred_element_type=F32)
                + jnp.dot(suffix_ones, lo, preferred_element_type=F32))

    def weights(z, incl, carry):
        log_a = z + incl
        if carry is not None:
            log_a = log_a + carry
        a = jnp.exp2(log_a)
        return a.astype(BF16), incl[0:1, :] if carry is None else carry + incl[0:1, :]

    def accumulate(p, j, a, first):
        start = pl.multiple_of(j * t, t)
        update = jnp.dot(vt_ref[p * LANES:(p + 1) * LANES, pl.ds(start, t)], a,
                         preferred_element_type=F32)
        if first:
            acc[p][...] = update
        else:
            acc[p][...] += update

    def tile(p, j, carry, diagonal):
        z = scores(p, j, diagonal)
        a, carry = weights(z, suffix_sums(*split_log_stay(z)), carry)
        accumulate(p, j, a, diagonal)
        return carry

    @pl.when(i == 0)
    def _():
        for p in pairs:
            tile(p, i, None, True)

    @pl.when(i > 0)
    def _():
        z_d = [scores(p, i, True) for p in pairs]
        z_p = [scores(p, i - 1, False) for p in pairs]
        split_d = [split_log_stay(z) for z in z_d]
        incl_d = [suffix_sums(*sp) for sp in split_d]
        split_p = [split_log_stay(z) for z in z_p]
        first = [weights(z_d[p], incl_d[p], None) for p in pairs]
        incl_p = [suffix_sums(*sp) for sp in split_p]
        for p in pairs:
            accumulate(p, i, first[p][0], True)
        second = [weights(z_p[p], incl_p[p], first[p][1]) for p in pairs]
        for p in pairs:
            accumulate(p, i - 1, second[p][0], False)
        carries = tuple(second[p][1] for p in pairs)

        def top_of(carries):
            return functools.reduce(jnp.maximum, [jnp.max(c) for c in carries])

        def cond(state):
            j, _, top_carry = state
            return (j >= 0) & (top_carry > SB_LOG_WEIGHT_FLOOR)

        def body(state):
            j, carries, _ = state
            carries = tuple(tile(p, j, carries[p], False) for p in pairs)
            return j - 1, carries, top_of(carries)

        lax.while_loop(cond, body, (i - 2, carries, top_of(carries)))

    for p in pairs:
        normed = []
        for h in range(2):
            o = acc[p][h * HEAD_DIM:(h + 1) * HEAD_DIM, h * t:(h + 1) * t]
            ms = jnp.mean(o * o, axis=0, keepdims=True)
            normed.append(o * lax.rsqrt(ms + NORM_EPS))
        o_ref[:, p * LANES:(p + 1) * LANES] = (
            jnp.concatenate(normed, axis=0).T * g_ref[...]).astype(o_ref.dtype)


def _sb_attention(qk, vt, g2):
    b, s, _ = qk.shape
    t = ATT_TILE
    width = SB_GROUP * LANES
    assert width == SB_WIDTH
    return pl.pallas_call(
        _sb_kernel,
        grid=(b, s // t),
        in_specs=[
            pl.BlockSpec((None, t, width), lambda bi, i: (bi, i, 0)),
            pl.BlockSpec((None, s, width), lambda bi, i: (bi, 0, 1), pipeline_mode=pl.Buffered(1)),
            pl.BlockSpec((None, width, s), lambda bi, i: (bi, 0, 0), pipeline_mode=pl.Buffered(1)),
            pl.BlockSpec((1, LANES), lambda bi, i: (0, 0)),
        ],
        out_specs=pl.BlockSpec((None, t, width), lambda bi, i: (bi, i, 0)),
        out_shape=jax.ShapeDtypeStruct((b, s, SB_WIDTH), BF16),
        scratch_shapes=([pltpu.VMEM((2 * t, LANES), BF16)] * SB_GROUP
                        + [pltpu.VMEM((LANES, 2 * t), F32)] * SB_GROUP),
        compiler_params=_params("parallel", "arbitrary"),
        name="sb_attention",
    )(qk, qk, vt, g2)


def _diff_kernel(q_ref, k_ref, vt_ref, bias_ref, lam_ref, g_ref, o_ref, *scratch, lambda_init):
    t = ATT_TILE
    tq = DIFF_Q_TILE
    i = pl.program_id(2)
    n_tiles = (tq // t) * (i + 1)
    heads = range(DIFF_GROUP)
    per_head = len(scratch) // DIFF_GROUP
    bufs = [scratch[h * per_head:(h + 1) * per_head] for h in heads]
    qq = [b[0] for b in bufs]
    even = [b[1:5] for b in bufs]
    odd = [b[5:9] for b in bufs]
    m_refs = [b[9] for b in bufs]
    acc_refs = [b[10] for b in bufs]
    lane = lax.broadcasted_iota(jnp.int32, (1, LANES), 1)
    ones_rows = jnp.ones((DIFF_SUM_ROWS, t), BF16)
    for h in heads:
        q = q_ref[:, h * LANES:(h + 1) * LANES]
        qq[h][0:tq, :] = jnp.where(lane < HEAD_DIM, q, jnp.zeros_like(q))
        qq[h][tq:2 * tq, :] = jnp.where(lane < HEAD_DIM, jnp.zeros_like(q), q)
        m_refs[h][...] = jnp.full_like(m_refs[h], MASKED_LOGIT)
        acc_refs[h][...] = jnp.zeros_like(acc_refs[h])

    def key_start(n):
        return pl.multiple_of((n_tiles - 1 - n) * t, t)

    def scores(n, parity, near=None):
        for h in heads:
            z_ref, zmax_ref, _, _ = parity[h]
            z = lax.dot_general(k_ref[pl.ds(key_start(n), t), h * LANES:(h + 1) * LANES],
                                qq[h][...], NT_DIMS, preferred_element_type=F32)
            if near is not None:
                z = z + bias_ref[h, near]
            z_ref[...] = z
            zmax_ref[...] = jnp.max(z, axis=0, keepdims=True)

    def softmax(parity):
        for h in heads:
            z_ref, zmax_ref, p_ref, a_ref = parity[h]
            m_prev = m_refs[h][...]
            m_new = jnp.maximum(m_prev, zmax_ref[...])
            a_ref[...] = jnp.exp2(m_prev - m_new)
            p_ref[...] = jnp.exp2(z_ref[...] - m_new).astype(BF16)
            m_refs[h][...] = m_new

    def values(n, parity):
        for h in heads:
            _, _, p_ref, a_ref = parity[h]
            vt = jnp.concatenate([vt_ref[h * LANES:(h + 1) * LANES, pl.ds(key_start(n), t)],
                                  ones_rows], axis=0)
            acc_refs[h][...] = a_ref[...] * acc_refs[h][...] + jnp.dot(
                vt, p_ref[...], preferred_element_type=F32)

    scores(0, even, 0)
    scores(1, odd, 1)
    softmax(even)

    @pl.when(i > 0)
    def _():
        scores(2, even, 2)
        softmax(odd)
        values(0, even)
        scores(3, odd)
        softmax(even)
        values(1, odd)

        def pair(k, carry):
            n = 2 * k + 2
            scores(n, even)
            softmax(odd)
            values(n - 2, even)
            scores(n + 1, odd)
            softmax(even)
            values(n - 1, odd)
            return carry

        lax.fori_loop(1, i, pair, 0)

    softmax(odd)
    values(n_tiles - 2, even)
    values(n_tiles - 1, odd)

    lp = lam_ref[...]
    lam = (jnp.exp(jnp.sum(lp[0:1] * lp[1:2], axis=-1, keepdims=True))
           - jnp.exp(jnp.sum(lp[2:3] * lp[3:4], axis=-1, keepdims=True)) + lambda_init)
    for h in heads:
        acc_ref = acc_refs[h]
        normalised = acc_ref[0:LANES, :] / acc_ref[LANES:LANES + 1, :]
        o = normalised[:, :tq] - lam * normalised[:, tq:]
        o = o * lax.rsqrt(jnp.mean(o * o, axis=0, keepdims=True) + NORM_EPS)
        o_ref[:, h * LANES:(h + 1) * LANES] = (
            o.T * g_ref[...] * (1.0 - lambda_init)).astype(o_ref.dtype)


def _diff_attention(qk, vt, bias_tiles, lam_params, g, lambda_init):
    b, s, _ = qk.shape
    t = ATT_TILE
    tq = DIFF_Q_TILE
    width = DIFF_GROUP * LANES
    q0 = 2 * SB_WIDTH // width
    v0 = SB_WIDTH // width
    groups = DIFF_HEADS // DIFF_GROUP
    tile_f32 = pltpu.VMEM((t, 2 * tq), F32)
    tile_bf16 = pltpu.VMEM((t, 2 * tq), BF16)
    row_f32 = pltpu.VMEM((1, 2 * tq), F32)
    stage = [tile_f32, row_f32, tile_bf16, row_f32]
    per_head = ([pltpu.VMEM((2 * tq, LANES), BF16)] + stage + stage
                + [row_f32, pltpu.VMEM((LANES + DIFF_SUM_ROWS, 2 * tq), F32)])
    return pl.pallas_call(
        functools.partial(_diff_kernel, lambda_init=lambda_init),
        grid=(b, groups, s // tq),
        in_specs=[
            pl.BlockSpec((None, tq, width), lambda bi, hg, i: (bi, i, q0 + hg)),
            pl.BlockSpec((None, s, width), lambda bi, hg, i: (bi, 0, q0 + groups + hg),
                         pipeline_mode=pl.Buffered(1)),
            pl.BlockSpec((None, width, s), lambda bi, hg, i: (bi, v0 + hg, 0),
                         pipeline_mode=pl.Buffered(1)),
            pl.BlockSpec((DIFF_GROUP, DIFF_NEAR_TILES, t, 2 * tq), lambda bi, hg, i: (hg, 0, 0, 0),
                         pipeline_mode=pl.Buffered(1)),
            pl.BlockSpec((4, HEAD_DIM), lambda bi, hg, i: (0, 0)),
            pl.BlockSpec((1, LANES), lambda bi, hg, i: (0, 0)),
        ],
        out_specs=pl.BlockSpec((None, tq, width), lambda bi, hg, i: (bi, i, hg)),
        out_shape=jax.ShapeDtypeStruct((b, s, DIFF_WIDTH), BF16),
        scratch_shapes=per_head * DIFF_GROUP,
        compiler_params=_params("parallel", "parallel", "arbitrary"),
        name="diff_attention",
    )(qk, qk, vt, bias_tiles, lam_params, g)


def _t5_causal_bucket(n):
    max_exact = N_BUCKETS // 2
    nf = jnp.maximum(n, 1).astype(F32)
    large = max_exact + (jnp.log(nf / max_exact) / math.log(MAX_DISTANCE / max_exact)
                         * (N_BUCKETS - max_exact)).astype(jnp.int32)
    large = jnp.minimum(large, N_BUCKETS - 1)
    return jnp.where(n < max_exact, n, large)


def _bias_tiles(rel_bias):
    t = ATT_TILE
    assert MAX_DISTANCE <= t and DIFF_NEAR_TILES == DIFF_Q_TILE // t + 1
    key = jnp.arange(t, dtype=jnp.int32)[:, None]
    qry = jnp.arange(DIFF_Q_TILE, dtype=jnp.int32)[None, :]
    dist = jnp.stack([qry - key - (DIFF_Q_TILE - (n + 1) * t) for n in range(DIFF_NEAR_TILES)])
    bucket = _t5_causal_bucket(jnp.maximum(dist, 0))
    table = (rel_bias.astype(F32) - rel_bias.astype(F32)[N_BUCKETS - 1][None, :]) * LOG2_E
    near = jnp.zeros((DIFF_HEADS,) + dist.shape, F32)
    for k in range(N_BUCKETS - 1):
        near = jnp.where(bucket[None] == k, table[k][:, None, None, None], near)
    near = jnp.where((dist >= 0)[None], near, MASKED_LOGIT)
    return jnp.concatenate([near, near], axis=-1)


def _route_top2(h, rw_ref):
    h_hi = h.astype(BF16)
    h_lo = (h - h_hi.astype(F32)).astype(BF16)
    both = jnp.dot(h_hi, rw_ref[...], preferred_element_type=F32)
    logits = (both[:, :LANES] + both[:, LANES:]
              + jnp.dot(h_lo, rw_ref[:, :LANES], preferred_element_type=F32))
    lane = lax.broadcasted_iota(jnp.int32, logits.shape, 1)
    neg = -jnp.inf
    lg = jnp.where(lane < N_EXPERTS, logits, neg)
    m1 = jnp.max(lg, axis=-1, keepdims=True)
    i1 = jnp.min(jnp.where(lg == m1, lane, LANES), axis=-1, keepdims=True)
    lg2 = jnp.where(lane == i1, neg, lg)
    m2 = jnp.max(lg2, axis=-1, keepdims=True)
    i2 = jnp.min(jnp.where(lg2 == m2, lane, LANES), axis=-1, keepdims=True)
    e = jnp.exp(m2 - m1)
    g1 = 1.0 / (1.0 + e)
    g2 = e * g1
    return jnp.where(lane == 0, i1.astype(F32),
                     jnp.where(lane == 1, i2.astype(F32),
                               jnp.where(lane == 2, g1, jnp.where(lane == 3, g2, 0.0))))


def _out_proj_route_kernel(x_ref, sb_ref, df_ref, wo_ref, g_ref, rw_ref, xo_ref, h_ref, route_ref):
    y = (jnp.dot(sb_ref[...], wo_ref[:SB_WIDTH, :], preferred_element_type=F32)
         + jnp.dot(df_ref[...], wo_ref[SB_WIDTH:, :], preferred_element_type=F32))
    xn = x_ref[...] + y
    xo_ref[...] = xn
    h = _rms(xn, g_ref[...])
    h_ref[...] = h
    route_ref[...] = _route_top2(h, rw_ref)


def _out_proj_route(x, sb, df, wo, g, router_w):
    n = x.shape[0]
    rows = lambda w: pl.BlockSpec((ROW_TILE, w), lambda i: (i, 0))
    whole = lambda a: pl.BlockSpec(a.shape, lambda i: (0, 0))
    return pl.pallas_call(
        _out_proj_route_kernel,
        grid=(n // ROW_TILE,),
        in_specs=[rows(D_MODEL), rows(SB_WIDTH), rows(DIFF_WIDTH), whole(wo), whole(g),
                  whole(router_w)],
        out_specs=[rows(D_MODEL), rows(D_MODEL), rows(LANES)],
        out_shape=[jax.ShapeDtypeStruct((n, D_MODEL), F32), jax.ShapeDtypeStruct((n, D_MODEL), F32),
                   jax.ShapeDtypeStruct((n, LANES), F32)],
        compiler_params=_params("parallel"),
        name="out_proj_route",
    )(x, sb, df, wo, g, router_w)


def _dense_ffn_kernel(x_ref, sb_ref, df_ref, wo_ref, g_ref, wg_ref, wu_ref, wd_ref, o_ref):
    y = (jnp.dot(sb_ref[...], wo_ref[:SB_WIDTH, :], preferred_element_type=F32)
         + jnp.dot(df_ref[...], wo_ref[SB_WIDTH:, :], preferred_element_type=F32))
    xn = x_ref[...] + y
    h = _rms(xn, g_ref[...]).astype(BF16)
    gate = jnp.dot(h, wg_ref[...], preferred_element_type=F32)
    up = jnp.dot(h, wu_ref[...], preferred_element_type=F32)
    act = (gate * jax.nn.sigmoid(gate) * up).astype(BF16)
    o_ref[...] = xn + jnp.dot(act, wd_ref[...], preferred_element_type=F32)


def _out_proj_dense_ffn(x, sb, df, wo, g, wg, wu, wd):
    n = x.shape[0]
    tm = ROW_TILE // 2
    rows = lambda w: pl.BlockSpec((tm, w), lambda i: (i, 0))
    whole = lambda a: pl.BlockSpec(a.shape, lambda i: (0, 0), pipeline_mode=pl.Buffered(1))
    return pl.pallas_call(
        _dense_ffn_kernel,
        grid=(n // tm,),
        in_specs=[rows(D_MODEL), rows(SB_WIDTH), rows(DIFF_WIDTH), whole(wo), whole(g),
                  whole(wg), whole(wu), whole(wd)],
        out_specs=rows(D_MODEL),
        out_shape=jax.ShapeDtypeStruct((n, D_MODEL), F32),
        compiler_params=_params("parallel"),
        name="out_proj_dense_ffn",
    )(x, sb, df, wo, g, wg, wu, wd)


def _expert_kernel(be_ref, src_ref, dst_ref, h_ref, wg_ref, wu_ref, wd_ref, y_ref,
                   xbuf, xb_ref, obuf, acc_ref, gsem, ssem):
    tm = MOE_TILE
    i = pl.program_id(0)
    f = pl.program_id(1)
    nb = pl.num_programs(0)
    nf = pl.num_programs(1)
    chunk = tm // MOE_FF_STEPS
    slot = i % 2
    other = 1 - slot

    def gather_row(r, row):
        return pltpu.make_async_copy(h_ref.at[pl.ds(row, 1), :], xbuf.at[pl.ds(r, 1), :], gsem)

    def scatter_row(slot_id, r, row):
        return pltpu.make_async_copy(obuf.at[slot_id, pl.ds(r, 1), :], y_ref.at[pl.ds(row, 1), :],
                                     ssem.at[slot_id])

    def wait_gather():
        pltpu.make_async_copy(h_ref.at[pl.ds(0, tm), :], xbuf.at[pl.ds(0, tm), :], gsem).wait()

    def wait_scatter(slot_id):
        pltpu.make_async_copy(obuf.at[slot_id], y_ref.at[pl.ds(0, tm), :], ssem.at[slot_id]).wait()

    @pl.when((i == 0) & (f == 0))
    def _():
        obuf[...] = jnp.zeros_like(obuf)
        xbuf[tm:, :] = jnp.zeros((SPARE_ROWS, D_MODEL), F32)

        def first(r, carry):
            gather_row(r, src_ref[r]).start()
            return carry

        lax.fori_loop(0, tm, first, 0)

    @pl.when(f == 0)
    def _():
        wait_gather()
        xb_ref[...] = xbuf[0:tm, :].astype(BF16)
        acc_ref[...] = jnp.zeros_like(acc_ref)

    x = xb_ref[...]
    n_slices = MOE_FF_TILE // MXU_WIDTH
    done = 0
    anchor = None
    for c in range(n_slices):
        cols = slice(c * MXU_WIDTH, (c + 1) * MXU_WIDTH)
        gate = jnp.dot(x, wg_ref[:, cols], preferred_element_type=F32)
        up = jnp.dot(x, wu_ref[:, cols], preferred_element_type=F32)
        if anchor is not None:
            up = up + anchor
        act = (gate * jax.nn.sigmoid(gate) * up).astype(BF16)
        acc_ref[...] += jnp.dot(act, wd_ref[cols, :], preferred_element_type=F32)
        if c < n_slices - 1:
            upto = chunk * (c + 1) // (n_slices - 1)
            for r in range(done, upto):
                rr = f * chunk + r
                gather_row(rr, src_ref[(i + 1) * tm + rr]).start()
            done = upto
            anchor = jnp.where(i < 0, xbuf[tm:tm + 1, 0:MXU_WIDTH], 0.0)

    for r in range(chunk):
        rr = f * chunk + r
        scatter_row(other, rr, dst_ref[i * tm + rr]).start()

    @pl.when(f == nf - 1)
    def _():
        @pl.when(i >= 1)
        def _():
            wait_scatter(slot)

        obuf[slot] = acc_ref[...]

        @pl.when(i == nb - 1)
        def _():
            def last(r, carry):
                scatter_row(slot, r, dst_ref[(i + 1) * tm + r]).start()
                return carry

            lax.fori_loop(0, tm, last, 0)
            wait_scatter(slot)
            wait_scatter(other)
            wait_gather()


def _expert_ffn(h, block_expert, src_rows, dst_rows, wg, wu, wd):
    n_slots = src_rows.shape[0]
    nb = n_slots // MOE_TILE - 1
    d_ff = wg.shape[-1]
    nf = d_ff // MOE_FF_TILE
    assert nf == MOE_FF_STEPS and MOE_TILE % nf == 0
    return pl.pallas_call(
        _expert_kernel,
        grid_spec=pltpu.PrefetchScalarGridSpec(
            num_scalar_prefetch=3,
            grid=(nb, nf),
            in_specs=[
                pl.BlockSpec(memory_space=pl.ANY),
                pl.BlockSpec((None, D_MODEL, MOE_FF_TILE), lambda i, f, be, s, d: (be[i], 0, f)),
                pl.BlockSpec((None, D_MODEL, MOE_FF_TILE), lambda i, f, be, s, d: (be[i], 0, f)),
                pl.BlockSpec((None, MOE_FF_TILE, D_MODEL), lambda i, f, be, s, d: (be[i], f, 0)),
            ],
            out_specs=pl.BlockSpec(memory_space=pl.ANY),
            scratch_shapes=[pltpu.VMEM((MOE_TILE + SPARE_ROWS, D_MODEL), F32),
                            pltpu.VMEM((MOE_TILE, D_MODEL), BF16),
                            pltpu.VMEM((2, MOE_TILE, D_MODEL), F32),
                            pltpu.VMEM((MOE_TILE, D_MODEL), F32),
                            pltpu.SemaphoreType.DMA(()), pltpu.SemaphoreType.DMA((2,))],
        ),
        out_shape=jax.ShapeDtypeStruct((n_slots, D_MODEL), F32),
        compiler_params=_params("arbitrary", "arbitrary"),
        name="expert_ffn",
    )(block_expert, src_rows, dst_rows, h, wg, wu, wd)


def _combine_kernel(x_ref, y0_ref, y1_ref, route_ref, g_ref, o_ref, *, final):
    r = route_ref[...]
    out = x_ref[...] + r[:, 2:3] * y0_ref[...] + r[:, 3:4] * y1_ref[...]
    if final:
        out = _rms(out, g_ref[...])
    o_ref[...] = out


def _combine(x, y, route, g, final):
    n = x.shape[0]
    second = n // ROW_TILE
    rows = lambda w: pl.BlockSpec((ROW_TILE, w), lambda i: (i, 0))
    return pl.pallas_call(
        functools.partial(_combine_kernel, final=final),
        grid=(n // ROW_TILE,),
        in_specs=[rows(D_MODEL), rows(D_MODEL),
                  pl.BlockSpec((ROW_TILE, D_MODEL), lambda i: (second + i, 0)),
                  rows(LANES), pl.BlockSpec((1, D_MODEL), lambda i: (0, 0))],
        out_specs=rows(D_MODEL),
        out_shape=jax.ShapeDtypeStruct((n, D_MODEL), F32),
        compiler_params=_params("parallel"),
        name="moe_combine",
    )(x, y, y, route, g)


def _invert_slots_kernel(slot_ref, out_ref):
    def clear(s, carry):
        out_ref[s] = -1
        return carry

    def put(a, carry):
        out_ref[slot_ref[a]] = a
        return carry

    lax.fori_loop(0, out_ref.shape[0], clear, 0, unroll=8)
    lax.fori_loop(0, slot_ref.shape[0], put, 0, unroll=8)


def _invert_slots(slot_of, n_slots):
    return pl.pallas_call(
        _invert_slots_kernel,
        in_specs=[pl.BlockSpec(memory_space=pltpu.SMEM)],
        out_specs=pl.BlockSpec(memory_space=pltpu.SMEM),
        out_shape=jax.ShapeDtypeStruct((n_slots,), jnp.int32),
        name="invert_slots",
    )(slot_of)


def _moe(x, h, route, wg, wu, wd, final_g, final):
    n = x.shape[0]
    n_assign = n * TOP_K
    n_blocks = n_assign // MOE_TILE + N_EXPERTS
    cap = n_blocks * MOE_TILE
    experts = route[:, :TOP_K].astype(jnp.int32).T.reshape(n_assign)
    onehot = (experts[:, None] == jnp.arange(N_EXPERTS, dtype=jnp.int32)[None, :]).astype(jnp.int32)
    running = jnp.cumsum(onehot, axis=0)
    rank = jnp.sum(running * onehot, axis=1) - 1
    counts = running[-1]
    padded = (counts + MOE_TILE - 1) // MOE_TILE * MOE_TILE
    pad_ends = jnp.cumsum(padded)
    pad_starts = pad_ends - padded
    slot_of = jnp.sum(onehot * pad_starts[None, :], axis=1) + rank
    assign_of = _invert_slots(slot_of.astype(jnp.int32), cap)
    empty = assign_of < 0
    spare = n_assign + jnp.cumsum(empty.astype(jnp.int32)) - 1
    dst = jnp.where(empty, spare, assign_of)
    src = jnp.where(empty, 0, assign_of % n)
    lead = cap + jnp.arange(MOE_TILE, dtype=jnp.int32)
    dst_rows = jnp.concatenate([lead, dst]).astype(jnp.int32)
    src_rows = jnp.concatenate([src, jnp.zeros((MOE_TILE,), jnp.int32)]).astype(jnp.int32)
    block_start = jnp.arange(n_blocks, dtype=jnp.int32) * MOE_TILE
    block_expert = jnp.minimum(
        jnp.sum((block_start[:, None] >= pad_ends[None, :]).astype(jnp.int32), axis=1),
        N_EXPERTS - 1).astype(jnp.int32)

    y = _expert_ffn(h, block_expert, src_rows, dst_rows, wg, wu, wd)
    return _combine(x, y, route, final_g, final)


def _final_norm_kernel(x_ref, g_ref, o_ref):
    o_ref[...] = _rms(x_ref[...], g_ref[...])


def _final_norm(x, g):
    n = x.shape[0]
    rows = pl.BlockSpec((ROW_TILE, D_MODEL), lambda i: (i, 0))
    return pl.pallas_call(
        _final_norm_kernel,
        grid=(n // ROW_TILE,),
        in_specs=[rows, pl.BlockSpec((1, D_MODEL), lambda i: (0, 0))],
        out_specs=rows,
        out_shape=jax.ShapeDtypeStruct((n, D_MODEL), F32),
        compiler_params=_params("parallel"),
        name="final_norm",
    )(x, g)


def kernel(x, w_in, w_out, attn_norm, ffn_norm, sb_out_norm, diff_subln, lambda_q1, lambda_k1,
           lambda_q2, lambda_k2, rel_bias, dense_w_gate, dense_w_up, dense_w_down, router_w,
           expert_w_gate, expert_w_up, expert_w_down, final_norm):
    b, s, d = x.shape
    depth = w_in.shape[0]
    n = b * s
    assert d == D_MODEL and s % DIFF_Q_TILE == 0 and s % ROW_TILE == 0
    assert (n * TOP_K) % MOE_TILE == 0

    scale = HEAD_DIM ** -0.5
    sb_q, sb_k, sb_v = 0, SB_WIDTH, 2 * SB_WIDTH
    df_q, df_k, df_v = 3 * SB_WIDTH, 3 * SB_WIDTH + DIFF_WIDTH, 3 * SB_WIDTH + 2 * DIFF_WIDTH
    bias_tiles = _bias_tiles(rel_bias)
    row2 = lambda v: v.astype(F32).reshape(1, -1)

    xf = x.reshape(n, d)
    for i in range(depth):
        last = i == depth - 1
        w = w_in[i]
        wqk = jnp.concatenate([w[:, sb_q:sb_k] * (scale * LOG2_E), w[:, sb_k:sb_v],
                               w[:, df_q:df_k] * (scale * LOG2_E), w[:, df_k:df_v]],
                              axis=1).astype(BF16)
        wvt = jnp.concatenate([w[:, sb_v:df_q], w[:, df_v:]], axis=1).T.astype(BF16)
        qk, vt = _norm_proj(xf, row2(attn_norm[i]), wqk, wvt, b, s)
        qk = qk.reshape(b, s, QK_WIDTH)
        sb = _sb_attention(qk, vt, row2(jnp.tile(sb_out_norm[i], 2)))
        lambda_init = 0.8 - 0.6 * math.exp(-0.3 * i)
        lam_params = jnp.stack([lambda_q1[i], lambda_k1[i], lambda_q2[i], lambda_k2[i]]).astype(F32)
        df = _diff_attention(qk, vt, bias_tiles, lam_params, row2(diff_subln[i]), lambda_init)
        sb = sb.reshape(n, SB_WIDTH)
        df = df.reshape(n, DIFF_WIDTH)
        wo = w_out[i].astype(BF16)
        j = i // 2
        if i % 2 == 0:
            xf = _out_proj_dense_ffn(xf, sb, df, wo, row2(ffn_norm[i]),
                                     dense_w_gate[j].astype(BF16), dense_w_up[j].astype(BF16),
                                     dense_w_down[j].astype(BF16))
            if last:
                xf = _final_norm(xf, row2(final_norm))
        else:
            rw = jnp.pad(router_w[j].astype(F32), ((0, 0), (0, LANES - N_EXPERTS)))
            rw_hi = rw.astype(BF16)
            rw_lo = (rw - rw_hi.astype(F32)).astype(BF16)
            xf, h, route = _out_proj_route(xf, sb, df, wo, row2(ffn_norm[i]),
                                           jnp.concatenate([rw_hi, rw_lo], axis=1))
            xf = _moe(xf, h, route, expert_w_gate[j].astype(BF16), expert_w_up[j].astype(BF16),
                      expert_w_down[j].astype(BF16), row2(final_norm), last)
    return xf.reshape(b, s, d)
```

```python
import functools
import math

import jax
import jax.numpy as jnp
from jax import lax
from jax.experimental import pallas as pl
from jax.experimental.pallas import tpu as pltpu

D_MODEL = 1024
HEAD_DIM = 64
SB_HEADS = 8
DIFF_HEADS = 4
SB_WIDTH = SB_HEADS * HEAD_DIM
DIFF_WIDTH = DIFF_HEADS * 2 * HEAD_DIM
QK_WIDTH = 2 * SB_WIDTH + 2 * DIFF_WIDTH
V_WIDTH = SB_WIDTH + DIFF_WIDTH
N_BUCKETS = 32
MAX_DISTANCE = 128
N_EXPERTS = 8
TOP_K = 2
NORM_EPS = 1e-6

LANES = 128
MXU_WIDTH = 256
SPARE_ROWS = 8
VMEM_LIMIT = 56 * 1024 * 1024

ROW_TILE = 512
ATT_TILE = 256
DIFF_Q_TILE = 512
DIFF_NEAR_TILES = 3
DIFF_GROUP = 4
SB_GROUP = 4
MOE_TILE = 512
MOE_FF_TILE = 1792
MOE_FF_STEPS = 2
SB_LOG_WEIGHT_FLOOR = -150.0
MASKED_LOGIT = -1e30
LOG2_E = math.log2(math.e)
DIFF_SUM_ROWS = 16

F32 = jnp.float32
BF16 = jnp.bfloat16
NT_DIMS = (((1,), (1,)), ((), ()))


def _params(*semantics):
    return pltpu.CompilerParams(dimension_semantics=semantics, vmem_limit_bytes=VMEM_LIMIT)


def _rms(x, g):
    return x * lax.rsqrt(jnp.mean(x * x, axis=-1, keepdims=True) + NORM_EPS) * g


def _norm_proj_kernel(x_ref, g_ref, wqk_ref, wvt_ref, qk_ref, vt_ref):
    h = _rms(x_ref[...], g_ref[...]).astype(BF16)
    for c in range(QK_WIDTH // D_MODEL):
        cols = slice(c * D_MODEL, (c + 1) * D_MODEL)
        qk_ref[:, cols] = jnp.dot(h, wqk_ref[:, cols], preferred_element_type=F32).astype(BF16)
    vt_ref[...] = lax.dot_general(wvt_ref[...], h, NT_DIMS,
                                  preferred_element_type=F32).astype(BF16)


def _norm_proj(x, g, wqk, wvt, batch, seq):
    n = x.shape[0]
    per_seq = seq // ROW_TILE
    whole = lambda a: pl.BlockSpec(a.shape, lambda b, i: (0, 0), pipeline_mode=pl.Buffered(1))
    return pl.pallas_call(
        _norm_proj_kernel,
        grid=(batch, per_seq),
        in_specs=[
            pl.BlockSpec((ROW_TILE, D_MODEL), lambda b, i: (b * per_seq + i, 0)),
            pl.BlockSpec((1, D_MODEL), lambda b, i: (0, 0)),
            whole(wqk), whole(wvt),
        ],
        out_specs=[
            pl.BlockSpec((ROW_TILE, QK_WIDTH), lambda b, i: (b * per_seq + i, 0)),
            pl.BlockSpec((None, V_WIDTH, ROW_TILE), lambda b, i: (b, 0, i)),
        ],
        out_shape=[jax.ShapeDtypeStruct((n, QK_WIDTH), BF16),
                   jax.ShapeDtypeStruct((batch, V_WIDTH, seq), BF16)],
        compiler_params=_params("parallel", "parallel"),
        name="norm_proj",
    )(x, g, wqk, wvt)


def _sb_kernel(q_ref, k_ref, vt_ref, g_ref, o_ref, *scratch):
    t = ATT_TILE
    i = pl.program_id(1)
    pairs = range(SB_GROUP)
    qq = scratch[:SB_GROUP]
    acc = scratch[SB_GROUP:]
    lane = lax.broadcasted_iota(jnp.int32, (1, LANES), 1)
    for p in pairs:
        q = q_ref[:, p * LANES:(p + 1) * LANES]
        qq[p][0:t, :] = jnp.where(lane < HEAD_DIM, q, jnp.zeros_like(q))
        qq[p][t:2 * t, :] = jnp.where(lane < HEAD_DIM, jnp.zeros_like(q), q)
    key = lax.broadcasted_iota(jnp.int32, (t, t), 0)
    pos = lax.broadcasted_iota(jnp.int32, (t, t), 1)
    suffix_ones = (pos >= key).astype(BF16)
    key2 = lax.broadcasted_iota(jnp.int32, (t, 2 * t), 0)
    qry2 = lax.broadcasted_iota(jnp.int32, (t, 2 * t), 1)
    strictly_causal = key2 < jnp.where(qry2 >= t, qry2 - t, qry2)

    def scores(p, j, diagonal):
        start = pl.multiple_of(j * t, t)
        z = lax.dot_general(k_ref[pl.ds(start, t), p * LANES:(p + 1) * LANES], qq[p][...],
                            NT_DIMS, preferred_element_type=F32)
        return jnp.where(strictly_causal, z, MASKED_LOGIT) if diagonal else z

    def split_log_stay(z):
        neg_z = -z
        log_stay = jnp.minimum(neg_z, 0.0) - jnp.log2(1.0 + jnp.exp2(jnp.minimum(z, neg_z)))
        hi = log_stay.astype(BF16)
        return hi, (log_stay - hi.astype(F32)).astype(BF16)

    def suffix_sums(hi, lo):
        return (jnp.dot(suffix_ones, hi, preferred_element_type=F32)
                + jnp.dot(suffix_ones, lo, preferred_element_type=F32))

    def weights(z, incl, carry):
        log_a = z + incl
        if carry is not None:
            log_a = log_a + carry
        a = jnp.exp2(log_a)
        return a.astype(BF16), incl[0:1, :] if carry is None else carry + incl[0:1, :]

    def accumulate(p, j, a, first):
        start = pl.multiple_of(j * t, t)
        update = jnp.dot(vt_ref[p * LANES:(p + 1) * LANES, pl.ds(start, t)], a,
                         preferred_element_type=F32)
        if first:
            acc[p][...] = update
        else:
            acc[p][...] += update

    def tile(p, j, carry, diagonal):
        z = scores(p, j, diagonal)
        a, carry = weights(z, suffix_sums(*split_log_stay(z)), carry)
        accumulate(p, j, a, diagonal)
        return carry

    @pl.when(i == 0)
    def _():
        for p in pairs:
            tile(p, i, None, True)

    @pl.when(i > 0)
    def _():
        z_d = [scores(p, i, True) for p in pairs]
        z_p = [scores(p, i - 1, False) for p in pairs]
        split_d = [split_log_stay(z) for z in z_d]
        incl_d = [suffix_sums(*sp) for sp in split_d]
        split_p = [split_log_stay(z) for z in z_p]
        first = [weights(z_d[p], incl_d[p], None) for p in pairs]
        incl_p = [suffix_sums(*sp) for sp in split_p]
        for p in pairs:
            accumulate(p, i, first[p][0], True)
        second = [weights(z_p[p], incl_p[p], first[p][1]) for p in pairs]
        for p in pairs:
            accumulate(p, i - 1, second[p][0], False)
        carries = tuple(second[p][1] for p in pairs)

        def top_of(carries):
            return functools.reduce(jnp.maximum, [jnp.max(c) for c in carries])

        def cond(state):
            j, _, top_carry = state
            return (j >= 0) & (top_carry > SB_LOG_WEIGHT_FLOOR)

        def body(state):
            j, carries, _ = state
            carries = tuple(tile(p, j, carries[p], False) for p in pairs)
            return j - 1, carries, top_of(carries)

        lax.while_loop(cond, body, (i - 2, carries, top_of(carries)))

    for p in pairs:
        normed = []
        for h in range(2):
            o = acc[p][h * HEAD_DIM:(h + 1) * HEAD_DIM, h * t:(h + 1) * t]
            ms = jnp.mean(o * o, axis=0, keepdims=True)
            normed.append(o * lax.rsqrt(ms + NORM_EPS))
        o_ref[:, p * LANES:(p + 1) * LANES] = (
            jnp.concatenate(normed, axis=0).T * g_ref[...]).astype(o_ref.dtype)


def _sb_attention(qk, vt, g2):
    b, s, _ = qk.shape
    t = ATT_TILE
    width = SB_GROUP * LANES
    assert width == SB_WIDTH
    return pl.pallas_call(
        _sb_kernel,
        grid=(b, s // t),
        in_specs=[
            pl.BlockSpec((None, t, width), lambda bi, i: (bi, i, 0)),
            pl.BlockSpec((None, s, width), lambda bi, i: (bi, 0, 1), pipeline_mode=pl.Buffered(1)),
            pl.BlockSpec((None, width, s), lambda bi, i: (bi, 0, 0), pipeline_mode=pl.Buffered(1)),
            pl.BlockSpec((1, LANES), lambda bi, i: (0, 0)),
        ],
        out_specs=pl.BlockSpec((None, t, width), lambda bi, i: (bi, i, 0)),
        out_shape=jax.ShapeDtypeStruct((b, s, SB_WIDTH), BF16),
        scratch_shapes=([pltpu.VMEM((2 * t, LANES), BF16)] * SB_GROUP
                        + [pltpu.VMEM((LANES, 2 * t), F32)] * SB_GROUP),
        compiler_params=_params("parallel", "arbitrary"),
        name="sb_attention",
    )(qk, qk, vt, g2)


def _diff_kernel(q_ref, k_ref, vt_ref, bias_ref, lam_ref, g_ref, o_ref, *scratch, lambda_init):
    t = ATT_TILE
    tq = DIFF_Q_TILE
    i = pl.program_id(2)
    n_tiles = (tq // t) * (i + 1)
    heads = range(DIFF_GROUP)
    per_head = len(scratch) // DIFF_GROUP
    bufs = [scratch[h * per_head:(h + 1) * per_head] for h in heads]
    qq = [b[0] for b in bufs]
    even = [b[1:5] for b in bufs]
    odd = [b[5:9] for b in bufs]
    m_refs = [b[9] for b in bufs]
    acc_refs = [b[10] for b in bufs]
    lane = lax.broadcasted_iota(jnp.int32, (1, LANES), 1)
    ones_rows = jnp.ones((DIFF_SUM_ROWS, t), BF16)
    for h in heads:
        q = q_ref[:, h * LANES:(h + 1) * LANES]
        qq[h][0:tq, :] = jnp.where(lane < HEAD_DIM, q, jnp.zeros_like(q))
        qq[h][tq:2 * tq, :] = jnp.where(lane < HEAD_DIM, jnp.zeros_like(q), q)
        m_refs[h][...] = jnp.full_like(m_refs[h], MASKED_LOGIT)
        acc_refs[h][...] = jnp.zeros_like(acc_refs[h])

    def key_start(n):
        return pl.multiple_of((n_tiles - 1 - n) * t, t)

    def scores(n, parity, near=None):
        for h in heads:
            z_ref, zmax_ref, _, _ = parity[h]
            z = lax.dot_general(k_ref[pl.ds(key_start(n), t), h * LANES:(h + 1) * LANES],
                                qq[h][...], NT_DIMS, preferred_element_type=F32)
            if near is not None:
                z = z + bias_ref[h, near]
            z_ref[...] = z
            zmax_ref[...] = jnp.max(z, axis=0, keepdims=True)

    def softmax(parity):
        for h in heads:
            z_ref, zmax_ref, p_ref, a_ref = parity[h]
            m_prev = m_refs[h][...]
            m_new = jnp.maximum(m_prev, zmax_ref[...])
            a_ref[...] = jnp.exp2(m_prev - m_new)
            p_ref[...] = jnp.exp2(z_ref[...] - m_new).astype(BF16)
            m_refs[h][...] = m_new

    def values(n, parity):
        for h in heads:
            _, _, p_ref, a_ref = parity[h]
            vt = jnp.concatenate([vt_ref[h * LANES:(h + 1) * LANES, pl.ds(key_start(n), t)],
                                  ones_rows], axis=0)
            acc_refs[h][...] = a_ref[...] * acc_refs[h][...] + jnp.dot(
                vt, p_ref[...], preferred_element_type=F32)

    scores(0, even, 0)
    scores(1, odd, 1)
    softmax(even)

    @pl.when(i > 0)
    def _():
        scores(2, even, 2)
        softmax(odd)
        values(0, even)
        scores(3, odd)
        softmax(even)
        values(1, odd)

        def pair(k, carry):
            n = 2 * k + 2
            scores(n, even)
            softmax(odd)
            values(n - 2, even)
            scores(n + 1, odd)
            softmax(even)
            values(n - 1, odd)
            return carry

        lax.fori_loop(1, i, pair, 0)

    softmax(odd)
    values(n_tiles - 2, even)
    values(n_tiles - 1, odd)

    lp = lam_ref[...]
    lam = (jnp.exp(jnp.sum(lp[0:1] * lp[1:2], axis=-1, keepdims=True))
           - jnp.exp(jnp.sum(lp[2:3] * lp[3:4], axis=-1, keepdims=True)) + lambda_init)
    for h in heads:
        acc_ref = acc_refs[h]
        normalised = acc_ref[0:LANES, :] / acc_ref[LANES:LANES + 1, :]
        o = normalised[:, :tq] - lam * normalised[:, tq:]
        o = o * lax.rsqrt(jnp.mean(o * o, axis=0, keepdims=True) + NORM_EPS)
        o_ref[:, h * LANES:(h + 1) * LANES] = (
            o.T * g_ref[...] * (1.0 - lambda_init)).astype(o_ref.dtype)


def _diff_attention(qk, vt, bias_tiles, lam_params, g, lambda_init):
    b, s, _ = qk.shape
    t = ATT_TILE
    tq = DIFF_Q_TILE
    width = DIFF_GROUP * LANES
    q0 = 2 * SB_WIDTH // width
    v0 = SB_WIDTH // width
    groups = DIFF_HEADS // DIFF_GROUP
    tile_f32 = pltpu.VMEM((t, 2 * tq), F32)
    tile_bf16 = pltpu.VMEM((t, 2 * tq), BF16)
    row_f32 = pltpu.VMEM((1, 2 * tq), F32)
    stage = [tile_f32, row_f32, tile_bf16, row_f32]
    per_head = ([pltpu.VMEM((2 * tq, LANES), BF16)] + stage + stage
                + [row_f32, pltpu.VMEM((LANES + DIFF_SUM_ROWS, 2 * tq), F32)])
    return pl.pallas_call(
        functools.partial(_diff_kernel, lambda_init=lambda_init),
        grid=(b, groups, s // tq),
        in_specs=[
            pl.BlockSpec((None, tq, width), lambda bi, hg, i: (bi, i, q0 + hg)),
            pl.BlockSpec((None, s, width), lambda bi, hg, i: (bi, 0, q0 + groups + hg),
                         pipeline_mode=pl.Buffered(1)),
            pl.BlockSpec((None, width, s), lambda bi, hg, i: (bi, v0 + hg, 0),
                         pipeline_mode=pl.Buffered(1)),
            pl.BlockSpec((DIFF_GROUP, DIFF_NEAR_TILES, t, 2 * tq), lambda bi, hg, i: (hg, 0, 0, 0),
                         pipeline_mode=pl.Buffered(1)),
            pl.BlockSpec((4, HEAD_DIM), lambda bi, hg, i: (0, 0)),
            pl.BlockSpec((1, LANES), lambda bi, hg, i: (0, 0)),
        ],
        out_specs=pl.BlockSpec((None, tq, width), lambda bi, hg, i: (bi, i, hg)),
        out_shape=jax.ShapeDtypeStruct((b, s, DIFF_WIDTH), BF16),
        scratch_shapes=per_head * DIFF_GROUP,
        compiler_params=_params("parallel", "parallel", "arbitrary"),
        name="diff_attention",
    )(qk, qk, vt, bias_tiles, lam_params, g)


def _t5_causal_bucket(n):
    max_exact = N_BUCKETS // 2
    nf = jnp.maximum(n, 1).astype(F32)
    large = max_exact + (jnp.log(nf / max_exact) / math.log(MAX_DISTANCE / max_exact)
                         * (N_BUCKETS - max_exact)).astype(jnp.int32)
    large = jnp.minimum(large, N_BUCKETS - 1)
    return jnp.where(n < max_exact, n, large)


def _bias_tiles(rel_bias):
    t = ATT_TILE
    assert MAX_DISTANCE <= t and DIFF_NEAR_TILES == DIFF_Q_TILE // t + 1
    key = jnp.arange(t, dtype=jnp.int32)[:, None]
    qry = jnp.arange(DIFF_Q_TILE, dtype=jnp.int32)[None, :]
    dist = jnp.stack([qry - key - (DIFF_Q_TILE - (n + 1) * t) for n in range(DIFF_NEAR_TILES)])
    bucket = _t5_causal_bucket(jnp.maximum(dist, 0))
    table = (rel_bias.astype(F32) - rel_bias.astype(F32)[N_BUCKETS - 1][None, :]) * LOG2_E
    near = jnp.zeros((DIFF_HEADS,) + dist.shape, F32)
    for k in range(N_BUCKETS - 1):
        near = jnp.where(bucket[None] == k, table[k][:, None, None, None], near)
    near = jnp.where((dist >= 0)[None], near, MASKED_LOGIT)
    return jnp.concatenate([near, near], axis=-1)


def _route_top2(h, rw_ref):
    h_hi = h.astype(BF16)
    h_lo = (h - h_hi.astype(F32)).astype(BF16)
    both = jnp.dot(h_hi, rw_ref[...], preferred_element_type=F32)
    logits = (both[:, :LANES] + both[:, LANES:]
              + jnp.dot(h_lo, rw_ref[:, :LANES], preferred_element_type=F32))
    lane = lax.broadcasted_iota(jnp.int32, logits.shape, 1)
    neg = -jnp.inf
    lg = jnp.where(lane < N_EXPERTS, logits, neg)
    m1 = jnp.max(lg, axis=-1, keepdims=True)
    i1 = jnp.min(jnp.where(lg == m1, lane, LANES), axis=-1, keepdims=True)
    lg2 = jnp.where(lane == i1, neg, lg)
    m2 = jnp.max(lg2, axis=-1, keepdims=True)
    i2 = jnp.min(jnp.where(lg2 == m2, lane, LANES), axis=-1, keepdims=True)
    e = jnp.exp(m2 - m1)
    g1 = 1.0 / (1.0 + e)
    g2 = e * g1
    return jnp.where(lane == 0, i1.astype(F32),
                     jnp.where(lane == 1, i2.astype(F32),
                               jnp.where(lane == 2, g1, jnp.where(lane == 3, g2, 0.0))))


def _out_proj_route_kernel(x_ref, sb_ref, df_ref, wo_ref, g_ref, rw_ref, xo_ref, h_ref, route_ref):
    y = (jnp.dot(sb_ref[...], wo_ref[:SB_WIDTH, :], preferred_element_type=F32)
         + jnp.dot(df_ref[...], wo_ref[SB_WIDTH:, :], preferred_element_type=F32))
    xn = x_ref[...] + y
    xo_ref[...] = xn
    h = _rms(xn, g_ref[...])
    h_ref[...] = h
    route_ref[...] = _route_top2(h, rw_ref)


def _out_proj_route(x, sb, df, wo, g, router_w):
    n = x.shape[0]
    rows = lambda w: pl.BlockSpec((ROW_TILE, w), lambda i: (i, 0))
    whole = lambda a: pl.BlockSpec(a.shape, lambda i: (0, 0))
    return pl.pallas_call(
        _out_proj_route_kernel,
        grid=(n // ROW_TILE,),
        in_specs=[rows(D_MODEL), rows(SB_WIDTH), rows(DIFF_WIDTH), whole(wo), whole(g),
                  whole(router_w)],
        out_specs=[rows(D_MODEL), rows(D_MODEL), rows(LANES)],
        out_shape=[jax.ShapeDtypeStruct((n, D_MODEL), F32), jax.ShapeDtypeStruct((n, D_MODEL), F32),
                   jax.ShapeDtypeStruct((n, LANES), F32)],
        compiler_params=_params("parallel"),
        name="out_proj_route",
    )(x, sb, df, wo, g, router_w)


def _dense_ffn_kernel(x_ref, sb_ref, df_ref, wo_ref, g_ref, wg_ref, wu_ref, wd_ref, o_ref):
    y = (jnp.dot(sb_ref[...], wo_ref[:SB_WIDTH, :], preferred_element_type=F32)
         + jnp.dot(df_ref[...], wo_ref[SB_WIDTH:, :], preferred_element_type=F32))
    xn = x_ref[...] + y
    h = _rms(xn, g_ref[...]).astype(BF16)
    gate = jnp.dot(h, wg_ref[...], preferred_element_type=F32)
    up = jnp.dot(h, wu_ref[...], preferred_element_type=F32)
    act = (gate * jax.nn.sigmoid(gate) * up).astype(BF16)
    o_ref[...] = xn + jnp.dot(act, wd_ref[...], preferred_element_type=F32)


def _out_proj_dense_ffn(x, sb, df, wo, g, wg, wu, wd):
    n = x.shape[0]
    tm = ROW_TILE // 2
    rows = lambda w: pl.BlockSpec((tm, w), lambda i: (i, 0))
    whole = lambda a: pl.BlockSpec(a.shape, lambda i: (0, 0), pipeline_mode=pl.Buffered(1))
    return pl.pallas_call(
        _dense_ffn_kernel,
        grid=(n // tm,),
        in_specs=[rows(D_MODEL), rows(SB_WIDTH), rows(DIFF_WIDTH), whole(wo), whole(g),
                  whole(wg), whole(wu), whole(wd)],
        out_specs=rows(D_MODEL),
        out_shape=jax.ShapeDtypeStruct((n, D_MODEL), F32),
        compiler_params=_params("parallel"),
        name="out_proj_dense_ffn",
    )(x, sb, df, wo, g, wg, wu, wd)


def _expert_kernel(be_ref, src_ref, dst_ref, h_ref, wg_ref, wu_ref, wd_ref, y_ref,
                   xbuf, xb_ref, obuf, acc_ref, gsem, ssem):
    tm = MOE_TILE
    i = pl.program_id(0)
    f = pl.program_id(1)
    nb = pl.num_programs(0)
    nf = pl.num_programs(1)
    chunk = tm // MOE_FF_STEPS
    slot = i % 2
    other = 1 - slot

    def gather_row(r, row):
        return pltpu.make_async_copy(h_ref.at[pl.ds(row, 1), :], xbuf.at[pl.ds(r, 1), :], gsem)

    def scatter_row(slot_id, r, row):
        return pltpu.make_async_copy(obuf.at[slot_id, pl.ds(r, 1), :], y_ref.at[pl.ds(row, 1), :],
                                     ssem.at[slot_id])

    def wait_gather():
        pltpu.make_async_copy(h_ref.at[pl.ds(0, tm), :], xbuf.at[pl.ds(0, tm), :], gsem).wait()

    def wait_scatter(slot_id):
        pltpu.make_async_copy(obuf.at[slot_id], y_ref.at[pl.ds(0, tm), :], ssem.at[slot_id]).wait()

    @pl.when((i == 0) & (f == 0))
    def _():
        obuf[...] = jnp.zeros_like(obuf)
        xbuf[tm:, :] = jnp.zeros((SPARE_ROWS, D_MODEL), F32)

        def first(r, carry):
            gather_row(r, src_ref[r]).start()
            return carry

        lax.fori_loop(0, tm, first, 0)

    @pl.when(f == 0)
    def _():
        wait_gather()
        xb_ref[...] = xbuf[0:tm, :].astype(BF16)
        acc_ref[...] = jnp.zeros_like(acc_ref)

    def step(first_row):
        x = xb_ref[...]
        n_slices = MOE_FF_TILE // MXU_WIDTH
        done = 0
        anchor = None
        for c in range(n_slices):
            cols = slice(c * MXU_WIDTH, (c + 1) * MXU_WIDTH)
            gate = jnp.dot(x, wg_ref[:, cols], preferred_element_type=F32)
            up = jnp.dot(x, wu_ref[:, cols], preferred_element_type=F32)
            if anchor is not None:
                up = up + anchor
            act = (gate * jax.nn.sigmoid(gate) * up).astype(BF16)
            acc_ref[...] += jnp.dot(act, wd_ref[cols, :], preferred_element_type=F32)
            if c < n_slices - 1:
                upto = chunk * (c + 1) // (n_slices - 1)
                for r in range(first_row + done, first_row + upto):
                    gather_row(r, src_ref[(i + 1) * tm + r]).start()
                done = upto
                anchor = jnp.where(i < 0, xbuf[tm:tm + 1, 0:MXU_WIDTH], 0.0)

        for r in range(first_row, first_row + chunk):
            scatter_row(other, r, dst_ref[i * tm + r]).start()

    for fs in range(MOE_FF_STEPS):
        pl.when(f == fs)(functools.partial(step, fs * chunk))

    @pl.when(f == nf - 1)
    def _():
        @pl.when(i >= 1)
        def _():
            wait_scatter(slot)

        obuf[slot] = acc_ref[...]

        @pl.when(i == nb - 1)
        def _():
            def last(r, carry):
                scatter_row(slot, r, dst_ref[(i + 1) * tm + r]).start()
                return carry

            lax.fori_loop(0, tm, last, 0)
            wait_scatter(slot)
            wait_scatter(other)
            wait_gather()


def _expert_ffn(h, block_expert, src_rows, dst_rows, wg, wu, wd):
    n_slots = src_rows.shape[0]
    nb = n_slots // MOE_TILE - 1
    d_ff = wg.shape[-1]
    nf = d_ff // MOE_FF_TILE
    assert nf == MOE_FF_STEPS and MOE_TILE % nf == 0
    return pl.pallas_call(
        _expert_kernel,
        grid_spec=pltpu.PrefetchScalarGridSpec(
            num_scalar_prefetch=3,
            grid=(nb, nf),
            in_specs=[
                pl.BlockSpec(memory_space=pl.ANY),
                pl.BlockSpec((None, D_MODEL, MOE_FF_TILE), lambda i, f, be, s, d: (be[i], 0, f)),
                pl.BlockSpec((None, D_MODEL, MOE_FF_TILE), lambda i, f, be, s, d: (be[i], 0, f)),
                pl.BlockSpec((None, MOE_FF_TILE, D_MODEL), lambda i, f, be, s, d: (be[i], f, 0)),
            ],
            out_specs=pl.BlockSpec(memory_space=pl.ANY),
            scratch_shapes=[pltpu.VMEM((MOE_TILE + SPARE_ROWS, D_MODEL), F32),
                            pltpu.VMEM((MOE_TILE, D_MODEL), BF16),
                            pltpu.VMEM((2, MOE_TILE, D_MODEL), F32),
                            pltpu.VMEM((MOE_TILE, D_MODEL), F32),
                            pltpu.SemaphoreType.DMA(()), pltpu.SemaphoreType.DMA((2,))],
        ),
        out_shape=jax.ShapeDtypeStruct((n_slots, D_MODEL), F32),
        compiler_params=_params("arbitrary", "arbitrary"),
        name="expert_ffn",
    )(block_expert, src_rows, dst_rows, h, wg, wu, wd)


def _combine_kernel(x_ref, y0_ref, y1_ref, route_ref, g_ref, o_ref, *, final):
    r = route_ref[...]
    out = x_ref[...] + r[:, 2:3] * y0_ref[...] + r[:, 3:4] * y1_ref[...]
    if final:
        out = _rms(out, g_ref[...])
    o_ref[...] = out


def _combine(x, y, route, g, final):
    n = x.shape[0]
    second = n // ROW_TILE
    rows = lambda w: pl.BlockSpec((ROW_TILE, w), lambda i: (i, 0))
    return pl.pallas_call(
        functools.partial(_combine_kernel, final=final),
        grid=(n // ROW_TILE,),
        in_specs=[rows(D_MODEL), rows(D_MODEL),
                  pl.BlockSpec((ROW_TILE, D_MODEL), lambda i: (second + i, 0)),
                  rows(LANES), pl.BlockSpec((1, D_MODEL), lambda i: (0, 0))],
        out_specs=rows(D_MODEL),
        out_shape=jax.ShapeDtypeStruct((n, D_MODEL), F32),
        compiler_params=_params("parallel"),
        name="moe_combine",
    )(x, y, y, route, g)


def _moe(x, h, route, wg, wu, wd, final_g, final):
    n = x.shape[0]
    n_assign = n * TOP_K
    n_blocks = n_assign // MOE_TILE + N_EXPERTS
    cap = n_blocks * MOE_TILE
    experts = route[:, :TOP_K].astype(jnp.int32).T.reshape(n_assign)
    onehot = (experts[:, None] == jnp.arange(N_EXPERTS, dtype=jnp.int32)[None, :]).astype(jnp.int32)
    running = jnp.cumsum(onehot, axis=0)
    rank = jnp.sum(running * onehot, axis=1) - 1
    counts = running[-1]
    padded = (counts + MOE_TILE - 1) // MOE_TILE * MOE_TILE
    pad_ends = jnp.cumsum(padded)
    pad_starts = pad_ends - padded
    slot_of = jnp.sum(onehot * pad_starts[None, :], axis=1) + rank
    assign_of = jnp.full((cap,), -1, jnp.int32).at[slot_of].set(
        jnp.arange(n_assign, dtype=jnp.int32))
    empty = assign_of < 0
    spare = n_assign + jnp.cumsum(empty.astype(jnp.int32)) - 1
    dst = jnp.where(empty, spare, assign_of)
    src = jnp.where(empty, 0, assign_of % n)
    lead = cap + jnp.arange(MOE_TILE, dtype=jnp.int32)
    dst_rows = jnp.concatenate([lead, dst]).astype(jnp.int32)
    src_rows = jnp.concatenate([src, jnp.zeros((MOE_TILE,), jnp.int32)]).astype(jnp.int32)
    block_start = jnp.arange(n_blocks, dtype=jnp.int32) * MOE_TILE
    block_expert = jnp.minimum(
        jnp.sum((block_start[:, None] >= pad_ends[None, :]).astype(jnp.int32), axis=1),
        N_EXPERTS - 1).astype(jnp.int32)

    y = _expert_ffn(h, block_expert, src_rows, dst_rows, wg, wu, wd)
    return _combine(x, y, route, final_g, final)


def _final_norm_kernel(x_ref, g_ref, o_ref):
    o_ref[...] = _rms(x_ref[...], g_ref[...])


def _final_norm(x, g):
    n = x.shape[0]
    rows = pl.BlockSpec((ROW_TILE, D_MODEL), lambda i: (i, 0))
    return pl.pallas_call(
        _final_norm_kernel,
        grid=(n // ROW_TILE,),
        in_specs=[rows, pl.BlockSpec((1, D_MODEL), lambda i: (0, 0))],
        out_specs=rows,
        out_shape=jax.ShapeDtypeStruct((n, D_MODEL), F32),
        compiler_params=_params("parallel"),
        name="final_norm",
    )(x, g)


def kernel(x, w_in, w_out, attn_norm, ffn_norm, sb_out_norm, diff_subln, lambda_q1, lambda_k1,
           lambda_q2, lambda_k2, rel_bias, dense_w_gate, dense_w_up, dense_w_down, router_w,
           expert_w_gate, expert_w_up, expert_w_down, final_norm):
    b, s, d = x.shape
    depth = w_in.shape[0]
    n = b * s
    assert d == D_MODEL and s % DIFF_Q_TILE == 0 and s % ROW_TILE == 0
    assert (n * TOP_K) % MOE_TILE == 0

    scale = HEAD_DIM ** -0.5
    sb_q, sb_k, sb_v = 0, SB_WIDTH, 2 * SB_WIDTH
    df_q, df_k, df_v = 3 * SB_WIDTH, 3 * SB_WIDTH + DIFF_WIDTH, 3 * SB_WIDTH + 2 * DIFF_WIDTH
    bias_tiles = _bias_tiles(rel_bias)
    row2 = lambda v: v.astype(F32).reshape(1, -1)

    xf = x.reshape(n, d)
    for i in range(depth):
        last = i == depth - 1
        w = w_in[i]
        wqk = jnp.concatenate([w[:, sb_q:sb_k] * (scale * LOG2_E), w[:, sb_k:sb_v],
                               w[:, df_q:df_k] * (scale * LOG2_E), w[:, df_k:df_v]],
                              axis=1).astype(BF16)
        wvt = jnp.concatenate([w[:, sb_v:df_q], w[:, df_v:]], axis=1).T.astype(BF16)
        qk, vt = _norm_proj(xf, row2(attn_norm[i]), wqk, wvt, b, s)
        qk = qk.reshape(b, s, QK_WIDTH)
        sb = _sb_attention(qk, vt, row2(jnp.tile(sb_out_norm[i], 2)))
        lambda_init = 0.8 - 0.6 * math.exp(-0.3 * i)
        lam_params = jnp.stack([lambda_q1[i], lambda_k1[i], lambda_q2[i], lambda_k2[i]]).astype(F32)
        df = _diff_attention(qk, vt, bias_tiles, lam_params, row2(diff_subln[i]), lambda_init)
        sb = sb.reshape(n, SB_WIDTH)
        df = df.reshape(n, DIFF_WIDTH)
        wo = w_out[i].astype(BF16)
        j = i // 2
        if i % 2 == 0:
            xf = _out_proj_dense_ffn(xf, sb, df, wo, row2(ffn_norm[i]),
                                     dense_w_gate[j].astype(BF16), dense_w_up[j].astype(BF16),
                                     dense_w_down[j].astype(BF16))
            if last:
                xf = _final_norm(xf, row2(final_norm))
        else:
            rw = jnp.pad(router_w[j].astype(F32), ((0, 0), (0, LANES - N_EXPERTS)))
            rw_hi = rw.astype(BF16)
            rw_lo = (rw - rw_hi.astype(F32)).astype(BF16)
            xf, h, route = _out_proj_route(xf, sb, df, wo, row2(ffn_norm[i]),
                                           jnp.concatenate([rw_hi, rw_lo], axis=1))
            xf = _moe(xf, h, route, expert_w_gate[j].astype(BF16), expert_w_up[j].astype(BF16),
                      expert_w_down[j].astype(BF16), row2(final_norm), last)
    return xf.reshape(b, s, d)
```

```python
import functools
import math

import jax
import jax.numpy as jnp
from jax import lax
from jax.experimental import pallas as pl
from jax.experimental.pallas import tpu as pltpu

D_MODEL = 1024
HEAD_DIM = 64
SB_HEADS = 8
DIFF_HEADS = 4
SB_WIDTH = SB_HEADS * HEAD_DIM
DIFF_WIDTH = DIFF_HEADS * 2 * HEAD_DIM
QK_WIDTH = 2 * SB_WIDTH + 2 * DIFF_WIDTH
V_WIDTH = SB_WIDTH + DIFF_WIDTH
N_BUCKETS = 32
MAX_DISTANCE = 128
N_EXPERTS = 8
TOP_K = 2
NORM_EPS = 1e-6

LANES = 128
MXU_WIDTH = 256
SPARE_ROWS = 8
VMEM_LIMIT = 56 * 1024 * 1024

ROW_TILE = 512
ATT_TILE = 256
DIFF_Q_TILE = 512
DIFF_NEAR_TILES = 3
DIFF_GROUP = 4
SB_GROUP = 4
MOE_TILE = 512
MOE_FF_TILE = 1792
MOE_FF_STEPS = 2
SB_LOG_WEIGHT_FLOOR = -150.0
MASKED_LOGIT = -1e30
LOG2_E = math.log2(math.e)
DIFF_SUM_ROWS = 16

F32 = jnp.float32
BF16 = jnp.bfloat16
NT_DIMS = (((1,), (1,)), ((), ()))


def _params(*semantics):
    return pltpu.CompilerParams(dimension_semantics=semantics, vmem_limit_bytes=VMEM_LIMIT)


def _rms(x, g):
    return x * lax.rsqrt(jnp.mean(x * x, axis=-1, keepdims=True) + NORM_EPS) * g


def _norm_proj_kernel(x_ref, g_ref, wqk_ref, wvt_ref, qk_ref, vt_ref):
    h = _rms(x_ref[...], g_ref[...]).astype(BF16)
    for c in range(QK_WIDTH // D_MODEL):
        cols = slice(c * D_MODEL, (c + 1) * D_MODEL)
        qk_ref[:, cols] = jnp.dot(h, wqk_ref[:, cols], preferred_element_type=F32).astype(BF16)
    vt_ref[...] = lax.dot_general(wvt_ref[...], h, NT_DIMS,
                                  preferred_element_type=F32).astype(BF16)


def _norm_proj(x, g, wqk, wvt, batch, seq):
    n = x.shape[0]
    per_seq = seq // ROW_TILE
    whole = lambda a: pl.BlockSpec(a.shape, lambda b, i: (0, 0), pipeline_mode=pl.Buffered(1))
    return pl.pallas_call(
        _norm_proj_kernel,
        grid=(batch, per_seq),
        in_specs=[
            pl.BlockSpec((ROW_TILE, D_MODEL), lambda b, i: (b * per_seq + i, 0)),
            pl.BlockSpec((1, D_MODEL), lambda b, i: (0, 0)),
            whole(wqk), whole(wvt),
        ],
        out_specs=[
            pl.BlockSpec((ROW_TILE, QK_WIDTH), lambda b, i: (b * per_seq + i, 0)),
            pl.BlockSpec((None, V_WIDTH, ROW_TILE), lambda b, i: (b, 0, i)),
        ],
        out_shape=[jax.ShapeDtypeStruct((n, QK_WIDTH), BF16),
                   jax.ShapeDtypeStruct((batch, V_WIDTH, seq), BF16)],
        compiler_params=_params("parallel", "parallel"),
        name="norm_proj",
    )(x, g, wqk, wvt)


def _sb_kernel(q_ref, k_ref, vt_ref, g_ref, o_ref, *scratch):
    t = ATT_TILE
    i = pl.program_id(1)
    pairs = range(SB_GROUP)
    qq = scratch[:SB_GROUP]
    acc = scratch[SB_GROUP:]
    lane = lax.broadcasted_iota(jnp.int32, (1, LANES), 1)
    for p in pairs:
        q = q_ref[:, p * LANES:(p + 1) * LANES]
        qq[p][0:t, :] = jnp.where(lane < HEAD_DIM, q, jnp.zeros_like(q))
        qq[p][t:2 * t, :] = jnp.where(lane < HEAD_DIM, jnp.zeros_like(q), q)
    key = lax.broadcasted_iota(jnp.int32, (t, t), 0)
    pos = lax.broadcasted_iota(jnp.int32, (t, t), 1)
    suffix_ones = (pos >= key).astype(BF16)
    key2 = lax.broadcasted_iota(jnp.int32, (t, 2 * t), 0)
    qry2 = lax.broadcasted_iota(jnp.int32, (t, 2 * t), 1)
    strictly_causal = key2 < jnp.where(qry2 >= t, qry2 - t, qry2)

    def scores(p, j, diagonal):
        start = pl.multiple_of(j * t, t)
        z = lax.dot_general(k_ref[pl.ds(start, t), p * LANES:(p + 1) * LANES], qq[p][...],
                            NT_DIMS, preferred_element_type=F32)
        return jnp.where(strictly_causal, z, MASKED_LOGIT) if diagonal else z

    def split_log_stay(z):
        neg_z = -z
        log_stay = jnp.minimum(neg_z, 0.0) - jnp.log2(1.0 + jnp.exp2(jnp.minimum(z, neg_z)))
        hi = log_stay.astype(BF16)
        return hi, (log_stay - hi.astype(F32)).astype(BF16)

    def suffix_sums(hi, lo):
        return (jnp.dot(suffix_ones, hi, preferred_element_type=F32)
                + jnp.dot(suffix_ones, lo, preferred_element_type=F32))

    def weights(z, incl, carry):
        log_a = z + incl
        if carry is not None:
            log_a = log_a + carry
        a = jnp.exp2(log_a)
        return a.astype(BF16), incl[0:1, :] if carry is None else carry + incl[0:1, :]

    def accumulate(p, j, a, first):
        start = pl.multiple_of(j * t, t)
        update = jnp.dot(vt_ref[p * LANES:(p + 1) * LANES, pl.ds(start, t)], a,
                         preferred_element_type=F32)
        if first:
            acc[p][...] = update
        else:
            acc[p][...] += update

    def tile(p, j, carry, diagonal):
        z = scores(p, j, diagonal)
        a, carry = weights(z, suffix_sums(*split_log_stay(z)), carry)
        accumulate(p, j, a, diagonal)
        return carry

    @pl.when(i == 0)
    def _():
        for p in pairs:
            tile(p, i, None, True)

    @pl.when(i > 0)
    def _():
        z_d = [scores(p, i, True) for p in pairs]
        z_p = [scores(p, i - 1, False) for p in pairs]
        split_d = [split_log_stay(z) for z in z_d]
        incl_d = [suffix_sums(*sp) for sp in split_d]
        split_p = [split_log_stay(z) for z in z_p]
        first = [weights(z_d[p], incl_d[p], None) for p in pairs]
        incl_p = [suffix_sums(*sp) for sp in split_p]
        for p in pairs:
            accumulate(p, i, first[p][0], True)
        second = [weights(z_p[p], incl_p[p], first[p][1]) for p in pairs]
        for p in pairs:
            accumulate(p, i - 1, second[p][0], False)
        carries = tuple(second[p][1] for p in pairs)

        def top_of(carries):
            return functools.reduce(jnp.maximum, [jnp.max(c) for c in carries])

        def cond(state):
            j, _, top_carry = state
            return (j >= 0) & (top_carry > SB_LOG_WEIGHT_FLOOR)

        def body(state):
            j, carries, _ = state
            carries = tuple(tile(p, j, carries[p], False) for p in pairs)
            return j - 1, carries, top_of(carries)

        lax.while_loop(cond, body, (i - 2, carries, top_of(carries)))

    for p in pairs:
        normed = []
        for h in range(2):
            o = acc[p][h * HEAD_DIM:(h + 1) * HEAD_DIM, h * t:(h + 1) * t]
            ms = jnp.mean(o * o, axis=0, keepdims=True)
            normed.append(o * lax.rsqrt(ms + NORM_EPS))
        o_ref[:, p * LANES:(p + 1) * LANES] = (
            jnp.concatenate(normed, axis=0).T * g_ref[...]).astype(o_ref.dtype)


def _sb_attention(qk, vt, g2):
    b, s, _ = qk.shape
    t = ATT_TILE
    width = SB_GROUP * LANES
    assert width == SB_WIDTH
    return pl.pallas_call(
        _sb_kernel,
        grid=(b, s // t),
        in_specs=[
            pl.BlockSpec((None, t, width), lambda bi, i: (bi, i, 0)),
            pl.BlockSpec((None, s, width), lambda bi, i: (bi, 0, 1), pipeline_mode=pl.Buffered(1)),
            pl.BlockSpec((None, width, s), lambda bi, i: (bi, 0, 0), pipeline_mode=pl.Buffered(1)),
            pl.BlockSpec((1, LANES), lambda bi, i: (0, 0)),
        ],
        out_specs=pl.BlockSpec((None, t, width), lambda bi, i: (bi, i, 0)),
        out_shape=jax.ShapeDtypeStruct((b, s, SB_WIDTH), BF16),
        scratch_shapes=([pltpu.VMEM((2 * t, LANES), BF16)] * SB_GROUP
                        + [pltpu.VMEM((LANES, 2 * t), F32)] * SB_GROUP),
        compiler_params=_params("parallel", "arbitrary"),
        name="sb_attention",
    )(qk, qk, vt, g2)


def _diff_kernel(q_ref, k_ref, vt_ref, bias_ref, lam_ref, g_ref, o_ref, *scratch, lambda_init):
    t = ATT_TILE
    tq = DIFF_Q_TILE
    i = pl.program_id(2)
    n_tiles = (tq // t) * (i + 1)
    heads = range(DIFF_GROUP)
    per_head = len(scratch) // DIFF_GROUP
    bufs = [scratch[h * per_head:(h + 1) * per_head] for h in heads]
    qq = [b[0] for b in bufs]
    even = [b[1:5] for b in bufs]
    odd = [b[5:9] for b in bufs]
    m_refs = [b[9] for b in bufs]
    acc_refs = [b[10] for b in bufs]
    lane = lax.broadcasted_iota(jnp.int32, (1, LANES), 1)
    ones_rows = jnp.ones((DIFF_SUM_ROWS, t), BF16)
    for h in heads:
        q = q_ref[:, h * LANES:(h + 1) * LANES]
        qq[h][0:tq, :] = jnp.where(lane < HEAD_DIM, q, jnp.zeros_like(q))
        qq[h][tq:2 * tq, :] = jnp.where(lane < HEAD_DIM, jnp.zeros_like(q), q)
        m_refs[h][...] = jnp.full_like(m_refs[h], MASKED_LOGIT)
        acc_refs[h][...] = jnp.zeros_like(acc_refs[h])

    def key_start(n):
        return pl.multiple_of((n_tiles - 1 - n) * t, t)

    def scores(n, parity, near=None):
        for h in heads:
            z_ref, zmax_ref, _, _ = parity[h]
            z = lax.dot_general(k_ref[pl.ds(key_start(n), t), h * LANES:(h + 1) * LANES],
                                qq[h][...], NT_DIMS, preferred_element_type=F32)
            if near is not None:
                z = z + bias_ref[h, near]
            z_ref[...] = z
            zmax_ref[...] = jnp.max(z, axis=0, keepdims=True)

    def softmax(parity):
        for h in heads:
            z_ref, zmax_ref, p_ref, a_ref = parity[h]
            m_prev = m_refs[h][...]
            m_new = jnp.maximum(m_prev, zmax_ref[...])
            a_ref[...] = jnp.exp2(m_prev - m_new)
            p_ref[...] = jnp.exp2(z_ref[...] - m_new).astype(BF16)
            m_refs[h][...] = m_new

    def values(n, parity):
        for h in heads:
            _, _, p_ref, a_ref = parity[h]
            vt = jnp.concatenate([vt_ref[h * LANES:(h + 1) * LANES, pl.ds(key_start(n), t)],
                                  ones_rows], axis=0)
            acc_refs[h][...] = a_ref[...] * acc_refs[h][...] + jnp.dot(
                vt, p_ref[...], preferred_element_type=F32)

    scores(0, even, 0)
    scores(1, odd, 1)
    softmax(even)

    @pl.when(i > 0)
    def _():
        scores(2, even, 2)
        softmax(odd)
        values(0, even)
        scores(3, odd)
        softmax(even)
        values(1, odd)

        def pair(k, carry):
            n = 2 * k + 2
            scores(n, even)
            softmax(odd)
            values(n - 2, even)
            scores(n + 1, odd)
            softmax(even)
            values(n - 1, odd)
            return carry

        lax.fori_loop(1, i, pair, 0)

    softmax(odd)
    values(n_tiles - 2, even)
    values(n_tiles - 1, odd)

    lp = lam_ref[...]
    lam = (jnp.exp(jnp.sum(lp[0:1] * lp[1:2], axis=-1, keepdims=True))
           - jnp.exp(jnp.sum(lp[2:3] * lp[3:4], axis=-1, keepdims=True)) + lambda_init)
    for h in heads:
        acc_ref = acc_refs[h]
        normalised = acc_ref[0:LANES, :] / acc_ref[LANES:LANES + 1, :]
        o = normalised[:, :tq] - lam * normalised[:, tq:]
        o = o * lax.rsqrt(jnp.mean(o * o, axis=0, keepdims=True) + NORM_EPS)
        o_ref[:, h * LANES:(h + 1) * LANES] = (
            o.T * g_ref[...] * (1.0 - lambda_init)).astype(o_ref.dtype)


def _diff_attention(qk, vt, bias_tiles, lam_params, g, lambda_init):
    b, s, _ = qk.shape
    t = ATT_TILE
    tq = DIFF_Q_TILE
    width = DIFF_GROUP * LANES
    q0 = 2 * SB_WIDTH // width
    v0 = SB_WIDTH // width
    groups = DIFF_HEADS // DIFF_GROUP
    tile_f32 = pltpu.VMEM((t, 2 * tq), F32)
    tile_bf16 = pltpu.VMEM((t, 2 * tq), BF16)
    row_f32 = pltpu.VMEM((1, 2 * tq), F32)
    stage = [tile_f32, row_f32, tile_bf16, row_f32]
    per_head = ([pltpu.VMEM((2 * tq, LANES), BF16)] + stage + stage
                + [row_f32, pltpu.VMEM((LANES + DIFF_SUM_ROWS, 2 * tq), F32)])
    return pl.pallas_call(
        functools.partial(_diff_kernel, lambda_init=lambda_init),
        grid=(b, groups, s // tq),
        in_specs=[
            pl.BlockSpec((None, tq, width), lambda bi, hg, i: (bi, i, q0 + hg)),
            pl.BlockSpec((None, s, width), lambda bi, hg, i: (bi, 0, q0 + groups + hg),
                         pipeline_mode=pl.Buffered(1)),
            pl.BlockSpec((None, width, s), lambda bi, hg, i: (bi, v0 + hg, 0),
                         pipeline_mode=pl.Buffered(1)),
            pl.BlockSpec((DIFF_GROUP, DIFF_NEAR_TILES, t, 2 * tq), lambda bi, hg, i: (hg, 0, 0, 0),
                         pipeline_mode=pl.Buffered(1)),
            pl.BlockSpec((4, HEAD_DIM), lambda bi, hg, i: (0, 0)),
            pl.BlockSpec((1, LANES), lambda bi, hg, i: (0, 0)),
        ],
        out_specs=pl.BlockSpec((None, tq, width), lambda bi, hg, i: (bi, i, hg)),
        out_shape=jax.ShapeDtypeStruct((b, s, DIFF_WIDTH), BF16),
        scratch_shapes=per_head * DIFF_GROUP,
        compiler_params=_params("parallel", "parallel", "arbitrary"),
        name="diff_attention",
    )(qk, qk, vt, bias_tiles, lam_params, g)


def _t5_causal_bucket(n):
    max_exact = N_BUCKETS // 2
    nf = jnp.maximum(n, 1).astype(F32)
    large = max_exact + (jnp.log(nf / max_exact) / math.log(MAX_DISTANCE / max_exact)
                         * (N_BUCKETS - max_exact)).astype(jnp.int32)
    large = jnp.minimum(large, N_BUCKETS - 1)
    return jnp.where(n < max_exact, n, large)


def _bias_tiles(rel_bias):
    t = ATT_TILE
    assert MAX_DISTANCE <= t and DIFF_NEAR_TILES == DIFF_Q_TILE // t + 1
    key = jnp.arange(t, dtype=jnp.int32)[:, None]
    qry = jnp.arange(DIFF_Q_TILE, dtype=jnp.int32)[None, :]
    dist = jnp.stack([qry - key - (DIFF_Q_TILE - (n + 1) * t) for n in range(DIFF_NEAR_TILES)])
    bucket = _t5_causal_bucket(jnp.maximum(dist, 0))
    table = (rel_bias.astype(F32) - rel_bias.astype(F32)[N_BUCKETS - 1][None, :]) * LOG2_E
    near = jnp.zeros((DIFF_HEADS,) + dist.shape, F32)
    for k in range(N_BUCKETS - 1):
        near = jnp.where(bucket[None] == k, table[k][:, None, None, None], near)
    near = jnp.where((dist >= 0)[None], near, MASKED_LOGIT)
    return jnp.concatenate([near, near], axis=-1)


def _route_top2(h, rw_ref):
    h_hi = h.astype(BF16)
    h_lo = (h - h_hi.astype(F32)).astype(BF16)
    both = jnp.dot(h_hi, rw_ref[...], preferred_element_type=F32)
    logits = (both[:, :LANES] + both[:, LANES:]
              + jnp.dot(h_lo, rw_ref[:, :LANES], preferred_element_type=F32))
    lane = lax.broadcasted_iota(jnp.int32, logits.shape, 1)
    neg = -jnp.inf
    lg = jnp.where(lane < N_EXPERTS, logits, neg)
    m1 = jnp.max(lg, axis=-1, keepdims=True)
    i1 = jnp.min(jnp.where(lg == m1, lane, LANES), axis=-1, keepdims=True)
    lg2 = jnp.where(lane == i1, neg, lg)
    m2 = jnp.max(lg2, axis=-1, keepdims=True)
    i2 = jnp.min(jnp.where(lg2 == m2, lane, LANES), axis=-1, keepdims=True)
    e = jnp.exp(m2 - m1)
    g1 = 1.0 / (1.0 + e)
    g2 = e * g1
    return jnp.where(lane == 0, i1.astype(F32),
                     jnp.where(lane == 1, i2.astype(F32),
                               jnp.where(lane == 2, g1, jnp.where(lane == 3, g2, 0.0))))


def _out_proj_route_kernel(x_ref, sb_ref, df_ref, wo_ref, g_ref, rw_ref, xo_ref, h_ref, route_ref):
    y = (jnp.dot(sb_ref[...], wo_ref[:SB_WIDTH, :], preferred_element_type=F32)
         + jnp.dot(df_ref[...], wo_ref[SB_WIDTH:, :], preferred_element_type=F32))
    xn = x_ref[...] + y
    xo_ref[...] = xn
    h = _rms(xn, g_ref[...])
    h_ref[...] = h
    route_ref[...] = _route_top2(h, rw_ref)


def _out_proj_route(x, sb, df, wo, g, router_w):
    n = x.shape[0]
    rows = lambda w: pl.BlockSpec((ROW_TILE, w), lambda i: (i, 0))
    whole = lambda a: pl.BlockSpec(a.shape, lambda i: (0, 0))
    return pl.pallas_call(
        _out_proj_route_kernel,
        grid=(n // ROW_TILE,),
        in_specs=[rows(D_MODEL), rows(SB_WIDTH), rows(DIFF_WIDTH), whole(wo), whole(g),
                  whole(router_w)],
        out_specs=[rows(D_MODEL), rows(D_MODEL), rows(LANES)],
        out_shape=[jax.ShapeDtypeStruct((n, D_MODEL), F32), jax.ShapeDtypeStruct((n, D_MODEL), F32),
                   jax.ShapeDtypeStruct((n, LANES), F32)],
        compiler_params=_params("parallel"),
        name="out_proj_route",
    )(x, sb, df, wo, g, router_w)


def _dense_ffn_kernel(x_ref, sb_ref, df_ref, wo_ref, g_ref, wg_ref, wu_ref, wd_ref, o_ref):
    y = (jnp.dot(sb_ref[...], wo_ref[:SB_WIDTH, :], preferred_element_type=F32)
         + jnp.dot(df_ref[...], wo_ref[SB_WIDTH:, :], preferred_element_type=F32))
    xn = x_ref[...] + y
    h = _rms(xn, g_ref[...]).astype(BF16)
    gate = jnp.dot(h, wg_ref[...], preferred_element_type=F32)
    up = jnp.dot(h, wu_ref[...], preferred_element_type=F32)
    act = (gate * jax.nn.sigmoid(gate) * up).astype(BF16)
    o_ref[...] = xn + jnp.dot(act, wd_ref[...], preferred_element_type=F32)


def _out_proj_dense_ffn(x, sb, df, wo, g, wg, wu, wd):
    n = x.shape[0]
    tm = ROW_TILE // 2
    rows = lambda w: pl.BlockSpec((tm, w), lambda i: (i, 0))
    whole = lambda a: pl.BlockSpec(a.shape, lambda i: (0, 0), pipeline_mode=pl.Buffered(1))
    return pl.pallas_call(
        _dense_ffn_kernel,
        grid=(n // tm,),
        in_specs=[rows(D_MODEL), rows(SB_WIDTH), rows(DIFF_WIDTH), whole(wo), whole(g),
                  whole(wg), whole(wu), whole(wd)],
        out_specs=rows(D_MODEL),
        out_shape=jax.ShapeDtypeStruct((n, D_MODEL), F32),
        compiler_params=_params("parallel"),
        name="out_proj_dense_ffn",
    )(x, sb, df, wo, g, wg, wu, wd)


def _expert_kernel(be_ref, src_ref, dst_ref, h_ref, wg_ref, wu_ref, wd_ref, y_ref,
                   xbuf, xb_ref, obuf, acc_ref, gsem, ssem):
    tm = MOE_TILE
    i = pl.program_id(0)
    f = pl.program_id(1)
    nb = pl.num_programs(0)
    nf = pl.num_programs(1)
    chunk = tm // MOE_FF_STEPS
    slot = i % 2
    other = 1 - slot

    def gather_row(r, row):
        return pltpu.make_async_copy(h_ref.at[pl.ds(row, 1), :], xbuf.at[pl.ds(r, 1), :], gsem)

    def scatter_row(slot_id, r, row):
        return pltpu.make_async_copy(obuf.at[slot_id, pl.ds(r, 1), :], y_ref.at[pl.ds(row, 1), :],
                                     ssem.at[slot_id])

    def wait_gather():
        pltpu.make_async_copy(h_ref.at[pl.ds(0, tm), :], xbuf.at[pl.ds(0, tm), :], gsem).wait()

    def wait_scatter(slot_id):
        pltpu.make_async_copy(obuf.at[slot_id], y_ref.at[pl.ds(0, tm), :], ssem.at[slot_id]).wait()

    @pl.when((i == 0) & (f == 0))
    def _():
        obuf[...] = jnp.zeros_like(obuf)
        xbuf[tm:, :] = jnp.zeros((SPARE_ROWS, D_MODEL), F32)

        def first(r, carry):
            gather_row(r, src_ref[r]).start()
            return carry

        lax.fori_loop(0, tm, first, 0)

    @pl.when(f == 0)
    def _():
        wait_gather()
        xb_ref[...] = xbuf[0:tm, :].astype(BF16)
        acc_ref[...] = jnp.zeros_like(acc_ref)

    def step(first_row):
        x = xb_ref[...]
        n_slices = MOE_FF_TILE // MXU_WIDTH
        done = 0
        anchor = None
        for c in range(n_slices):
            cols = slice(c * MXU_WIDTH, (c + 1) * MXU_WIDTH)
            gate = jnp.dot(x, wg_ref[:, cols], preferred_element_type=F32)
            up = jnp.dot(x, wu_ref[:, cols], preferred_element_type=F32)
            if anchor is not None:
                up = up + anchor
            act = (gate * jax.nn.sigmoid(gate) * up).astype(BF16)
            acc_ref[...] += jnp.dot(act, wd_ref[cols, :], preferred_element_type=F32)
            if c < n_slices - 1:
                upto = chunk * (c + 1) // (n_slices - 1)
                for r in range(first_row + done, first_row + upto):
                    gather_row(r, src_ref[(i + 1) * tm + r]).start()
                done = upto
                anchor = jnp.where(i < 0, xbuf[tm:tm + 1, 0:MXU_WIDTH], 0.0)

        for r in range(first_row, first_row + chunk):
            scatter_row(other, r, dst_ref[i * tm + r]).start(priority=r % 2)

    for fs in range(MOE_FF_STEPS):
        pl.when(f == fs)(functools.partial(step, fs * chunk))

    @pl.when(f == nf - 1)
    def _():
        @pl.when(i >= 1)
        def _():
            wait_scatter(slot)

        obuf[slot] = acc_ref[...]

        @pl.when(i == nb - 1)
        def _():
            def last(r, carry):
                scatter_row(slot, r, dst_ref[(i + 1) * tm + r]).start()
                return carry

            lax.fori_loop(0, tm, last, 0)
            wait_scatter(slot)
            wait_scatter(other)
            wait_gather()


def _expert_ffn(h, block_expert, src_rows, dst_rows, wg, wu, wd):
    n_slots = src_rows.shape[0]
    nb = n_slots // MOE_TILE - 1
    d_ff = wg.shape[-1]
    nf = d_ff // MOE_FF_TILE
    assert nf == MOE_FF_STEPS and MOE_TILE % nf == 0
    return pl.pallas_call(
        _expert_kernel,
        grid_spec=pltpu.PrefetchScalarGridSpec(
            num_scalar_prefetch=3,
            grid=(nb, nf),
            in_specs=[
                pl.BlockSpec(memory_space=pl.ANY),
                pl.BlockSpec((None, D_MODEL, MOE_FF_TILE), lambda i, f, be, s, d: (be[i], 0, f)),
                pl.BlockSpec((None, D_MODEL, MOE_FF_TILE), lambda i, f, be, s, d: (be[i], 0, f)),
                pl.BlockSpec((None, MOE_FF_TILE, D_MODEL), lambda i, f, be, s, d: (be[i], f, 0)),
            ],
            out_specs=pl.BlockSpec(memory_space=pl.ANY),
            scratch_shapes=[pltpu.VMEM((MOE_TILE + SPARE_ROWS, D_MODEL), F32),
                            pltpu.VMEM((MOE_TILE, D_MODEL), BF16),
                            pltpu.VMEM((2, MOE_TILE, D_MODEL), F32),
                            pltpu.VMEM((MOE_TILE, D_MODEL), F32),
                            pltpu.SemaphoreType.DMA(()), pltpu.SemaphoreType.DMA((2,))],
        ),
        out_shape=jax.ShapeDtypeStruct((n_slots, D_MODEL), F32),
        compiler_params=_params("arbitrary", "arbitrary"),
        name="expert_ffn",
    )(block_expert, src_rows, dst_rows, h, wg, wu, wd)


def _combine_kernel(x_ref, y0_ref, y1_ref, route_ref, g_ref, o_ref, *, final):
    r = route_ref[...]
    out = x_ref[...] + r[:, 2:3] * y0_ref[...] + r[:, 3:4] * y1_ref[...]
    if final:
        out = _rms(out, g_ref[...])
    o_ref[...] = out


def _combine(x, y, route, g, final):
    n = x.shape[0]
    second = n // ROW_TILE
    rows = lambda w: pl.BlockSpec((ROW_TILE, w), lambda i: (i, 0))
    return pl.pallas_call(
        functools.partial(_combine_kernel, final=final),
        grid=(n // ROW_TILE,),
        in_specs=[rows(D_MODEL), rows(D_MODEL),
                  pl.BlockSpec((ROW_TILE, D_MODEL), lambda i: (second + i, 0)),
                  rows(LANES), pl.BlockSpec((1, D_MODEL), lambda i: (0, 0))],
        out_specs=rows(D_MODEL),
        out_shape=jax.ShapeDtypeStruct((n, D_MODEL), F32),
        compiler_params=_params("parallel"),
        name="moe_combine",
    )(x, y, y, route, g)


def _moe(x, h, route, wg, wu, wd, final_g, final):
    n = x.shape[0]
    n_assign = n * TOP_K
    n_blocks = n_assign // MOE_TILE + N_EXPERTS
    cap = n_blocks * MOE_TILE
    experts = route[:, :TOP_K].astype(jnp.int32).T.reshape(n_assign)
    onehot = (experts[:, None] == jnp.arange(N_EXPERTS, dtype=jnp.int32)[None, :]).astype(jnp.int32)
    running = jnp.cumsum(onehot, axis=0)
    rank = jnp.sum(running * onehot, axis=1) - 1
    counts = running[-1]
    padded = (counts + MOE_TILE - 1) // MOE_TILE * MOE_TILE
    pad_ends = jnp.cumsum(padded)
    pad_starts = pad_ends - padded
    slot_of = jnp.sum(onehot * pad_starts[None, :], axis=1) + rank
    assign_of = jnp.full((cap,), -1, jnp.int32).at[slot_of].set(
        jnp.arange(n_assign, dtype=jnp.int32))
    empty = assign_of < 0
    spare = n_assign + jnp.cumsum(empty.astype(jnp.int32)) - 1
    dst = jnp.where(empty, spare, assign_of)
    src = jnp.where(empty, 0, assign_of % n)
    lead = cap + jnp.arange(MOE_TILE, dtype=jnp.int32)
    dst_rows = jnp.concatenate([lead, dst]).astype(jnp.int32)
    src_rows = jnp.concatenate([src, jnp.zeros((MOE_TILE,), jnp.int32)]).astype(jnp.int32)
    block_start = jnp.arange(n_blocks, dtype=jnp.int32) * MOE_TILE
    block_expert = jnp.minimum(
        jnp.sum((block_start[:, None] >= pad_ends[None, :]).astype(jnp.int32), axis=1),
        N_EXPERTS - 1).astype(jnp.int32)

    y = _expert_ffn(h, block_expert, src_rows, dst_rows, wg, wu, wd)
    return _combine(x, y, route, final_g, final)


def _final_norm_kernel(x_ref, g_ref, o_ref):
    o_ref[...] = _rms(x_ref[...], g_ref[...])


def _final_norm(x, g):
    n = x.shape[0]
    rows = pl.BlockSpec((ROW_TILE, D_MODEL), lambda i: (i, 0))
    return pl.pallas_call(
        _final_norm_kernel,
        grid=(n // ROW_TILE,),
        in_specs=[rows, pl.BlockSpec((1, D_MODEL), lambda i: (0, 0))],
        out_specs=rows,
        out_shape=jax.ShapeDtypeStruct((n, D_MODEL), F32),
        compiler_params=_params("parallel"),
        name="final_norm",
    )(x, g)


def kernel(x, w_in, w_out, attn_norm, ffn_norm, sb_out_norm, diff_subln, lambda_q1, lambda_k1,
           lambda_q2, lambda_k2, rel_bias, dense_w_gate, dense_w_up, dense_w_down, router_w,
           expert_w_gate, expert_w_up, expert_w_down, final_norm):
    b, s, d = x.shape
    depth = w_in.shape[0]
    n = b * s
    assert d == D_MODEL and s % DIFF_Q_TILE == 0 and s % ROW_TILE == 0
    assert (n * TOP_K) % MOE_TILE == 0

    scale = HEAD_DIM ** -0.5
    sb_q, sb_k, sb_v = 0, SB_WIDTH, 2 * SB_WIDTH
    df_q, df_k, df_v = 3 * SB_WIDTH, 3 * SB_WIDTH + DIFF_WIDTH, 3 * SB_WIDTH + 2 * DIFF_WIDTH
    bias_tiles = _bias_tiles(rel_bias)
    row2 = lambda v: v.astype(F32).reshape(1, -1)

    xf = x.reshape(n, d)
    for i in range(depth):
        last = i == depth - 1
        w = w_in[i]
        wqk = jnp.concatenate([w[:, sb_q:sb_k] * (scale * LOG2_E), w[:, sb_k:sb_v],
                               w[:, df_q:df_k] * (scale * LOG2_E), w[:, df_k:df_v]],
                              axis=1).astype(BF16)
        wvt = jnp.concatenate([w[:, sb_v:df_q], w[:, df_v:]], axis=1).T.astype(BF16)
        qk, vt = _norm_proj(xf, row2(attn_norm[i]), wqk, wvt, b, s)
        qk = qk.reshape(b, s, QK_WIDTH)
        sb = _sb_attention(qk, vt, row2(jnp.tile(sb_out_norm[i], 2)))
        lambda_init = 0.8 - 0.6 * math.exp(-0.3 * i)
        lam_params = jnp.stack([lambda_q1[i], lambda_k1[i], lambda_q2[i], lambda_k2[i]]).astype(F32)
        df = _diff_attention(qk, vt, bias_tiles, lam_params, row2(diff_subln[i]), lambda_init)
        sb = sb.reshape(n, SB_WIDTH)
        df = df.reshape(n, DIFF_WIDTH)
        wo = w_out[i].astype(BF16)
        j = i // 2
        if i % 2 == 0:
            xf = _out_proj_dense_ffn(xf, sb, df, wo, row2(ffn_norm[i]),
                                     dense_w_gate[j].astype(BF16), dense_w_up[j].astype(BF16),
                                     dense_w_down[j].astype(BF16))
            if last:
                xf = _final_norm(xf, row2(final_norm))
        else:
            rw = jnp.pad(router_w[j].astype(F32), ((0, 0), (0, LANES - N_EXPERTS)))
            rw_hi = rw.astype(BF16)
            rw_lo = (rw - rw_hi.astype(F32)).astype(BF16)
            xf, h, route = _out_proj_route(xf, sb, df, wo, row2(ffn_norm[i]),
                                           jnp.concatenate([rw_hi, rw_lo], axis=1))
            xf = _moe(xf, h, route, expert_w_gate[j].astype(BF16), expert_w_up[j].astype(BF16),
                      expert_w_down[j].astype(BF16), row2(final_norm), last)
    return xf.reshape(b, s, d)
```
